```python
import math
import jax, jax.numpy as jnp
from jax import lax
import numpy as np

D_MODEL = 2048
BATCH = 2
SEQ = 4096
DEPTH = 4

N_EVEN = (DEPTH + 1) // 2
N_ODD = DEPTH // 2

SB_HEADS = 8
SB_HEAD_DIM = 128
SB_WIDTH = SB_HEADS * SB_HEAD_DIM
SB_BLOCK = 128
GLA_HEADS = 8
GLA_DK = 64
GLA_DV = 128
GLA_K_WIDTH = GLA_HEADS * GLA_DK
GLA_V_WIDTH = GLA_HEADS * GLA_DV
GLA_GATE_RANK = 16
GLA_GATE_TAU = 16.0
GLA_CHUNK = 64
IN_SPLITS = [SB_WIDTH, 2 * SB_WIDTH, 3 * SB_WIDTH,
             3 * SB_WIDTH + GLA_K_WIDTH,
             3 * SB_WIDTH + 2 * GLA_K_WIDTH,
             3 * SB_WIDTH + 2 * GLA_K_WIDTH + GLA_V_WIDTH,
             3 * SB_WIDTH + 2 * GLA_K_WIDTH + 2 * GLA_V_WIDTH]
IN_WIDTH = 3 * SB_WIDTH + 2 * GLA_K_WIDTH + 2 * GLA_V_WIDTH + GLA_GATE_RANK
MIX_WIDTH = SB_WIDTH + GLA_V_WIDTH
S5_GROUP = 16
S5_GROUPS = D_MODEL // S5_GROUP
S5_STATE = 64
S5_DT_MIN = 1e-3
S5_DT_MAX = 1e-1
D_FF = ((8 * D_MODEL + 3 * 256 - 1) // (3 * 256)) * 256
NORM_EPS = 1e-6

kernel_name = "hybrid_stickbreak_gla_s5_trunk"


def rms_norm(x, g):
    xf = x.astype(jnp.float32)
    return xf * lax.rsqrt(jnp.mean(xf * xf, axis=-1, keepdims=True) + NORM_EPS) * g.astype(jnp.float32)


def split_heads(t, n_heads):
    b_, l_, w_ = t.shape
    return t.reshape(b_, l_, n_heads, w_ // n_heads).transpose(0, 2, 1, 3)


def merge_heads(t):
    b_, h_, l_, d_ = t.shape
    return t.transpose(0, 2, 1, 3).reshape(b_, l_, h_ * d_)


def stick_breaking_attention(q, k, v):
    b_, h_, L, d = q.shape
    nblk = L // SB_BLOCK
    scale = d ** -0.5
    q_blocks = q.reshape(b_, h_, nblk, SB_BLOCK, d).transpose(2, 0, 1, 3, 4)
    starts = jnp.arange(nblk, dtype=jnp.int32) * SB_BLOCK
    kpos = jnp.arange(L, dtype=jnp.int32)

    def one_block(args):
        qb, start = args
        qpos = start + jnp.arange(SB_BLOCK, dtype=jnp.int32)
        mask = kpos[None, :] < qpos[:, None]
        z = jnp.einsum('bhqd,bhkd->bhqk', qb, k) * scale
        log_keep = jnp.where(mask, jax.nn.log_sigmoid(-z), 0.0)
        later = lax.cumsum(log_keep, axis=3, reverse=True) - log_keep
        w = jnp.where(mask, jnp.exp(jax.nn.log_sigmoid(z) + later), 0.0)
        return jnp.einsum('bhqk,bhkd->bhqd', w, v)

    out = lax.map(one_block, (q_blocks, starts))
    return out.transpose(1, 2, 0, 3, 4).reshape(b_, h_, L, d)


def gla_chunked(q, k, v, log_a):
    b_, h_, L, dk = q.shape
    dv = v.shape[-1]
    nc = L // GLA_CHUNK

    def to_chunks(t):
        return t.reshape(b_, h_, nc, GLA_CHUNK, t.shape[-1]).transpose(2, 0, 1, 3, 4)

    tpos = jnp.arange(GLA_CHUNK)
    incl = (tpos[None, :] <= tpos[:, None])[:, :, None]

    def step(S, xs):
        qc, kc, vc, lac = xs
        cum = jnp.cumsum(lac, axis=2)
        cum_last = cum[:, :, -1:, :]
        inter = jnp.einsum('bhtd,bhde->bhte', qc * jnp.exp(cum), S)
        diff = cum[:, :, :, None, :] - cum[:, :, None, :, :]
        decay = jnp.exp(jnp.where(incl, diff, -jnp.inf))
        scores = jnp.einsum('bhtsd,bhsd->bhts', qc[:, :, :, None, :] * decay, kc)
        intra = jnp.einsum('bhts,bhse->bhte', scores, vc)
        S_new = jnp.exp(cum_last[:, :, 0, :, None]) * S + jnp.einsum(
            'bhsd,bhse->bhde', kc * jnp.exp(cum_last - cum), vc)
        return S_new, inter + intra

    S0 = jnp.zeros((b_, h_, dk, dv), jnp.float32)
    _, out = lax.scan(step, S0, (to_chunks(q), to_chunks(k), to_chunks(v), to_chunks(log_a)))
    return out.transpose(1, 2, 0, 3, 4).reshape(b_, h_, L, dv)


def sb_gla_mixer(h, w_in, w_gate_up, b_gate, gla_norm_gain, w_out):
    proj = (h @ w_in).astype(jnp.float32)
    sb_q, sb_k, sb_v, g_q, g_k, g_v, g_r, g_lr = jnp.split(proj, IN_SPLITS, axis=-1)
    o_sb = stick_breaking_attention(split_heads(sb_q, SB_HEADS),
                                    split_heads(sb_k, SB_HEADS),
                                    split_heads(sb_v, SB_HEADS))
    o_sb = merge_heads(o_sb)
    gate_logits = g_lr @ w_gate_up.astype(jnp.float32) + b_gate.astype(jnp.float32)
    log_a = jax.nn.log_sigmoid(gate_logits) / GLA_GATE_TAU
    o_gla = gla_chunked(split_heads(g_q, GLA_HEADS) * (GLA_DK ** -0.5),
                        split_heads(g_k, GLA_HEADS),
                        split_heads(g_v, GLA_HEADS),
                        split_heads(log_a, GLA_HEADS))
    o_gla = o_gla * lax.rsqrt(jnp.mean(o_gla * o_gla, axis=-1, keepdims=True) + NORM_EPS)
    o_gla = merge_heads(o_gla) * gla_norm_gain.astype(jnp.float32) * jax.nn.silu(g_r)
    return jnp.concatenate([o_sb, o_gla], axis=-1) @ w_out


def s5_mixer(h, lam_re, lam_im, log_step, b_re, b_im, c_re, c_im, d_skip, w_glu):
    b_, L, _ = h.shape
    u = h.astype(jnp.float32).reshape(b_, L, S5_GROUPS, S5_GROUP)
    lam_re = lam_re.astype(jnp.float32)
    lam_im = lam_im.astype(jnp.float32)
    dt = jnp.exp(log_step.astype(jnp.float32))[:, None]
    mag = jnp.exp(dt * lam_re)
    ang = dt * lam_im
    lb_re, lb_im = mag * jnp.cos(ang), mag * jnp.sin(ang)
    den = lam_re * lam_re + lam_im * lam_im
    nr, ni = lb_re - 1.0, lb_im
    cr = (nr * lam_re + ni * lam_im) / den
    ci = (ni * lam_re - nr * lam_im) / den
    b_re = b_re.astype(jnp.float32)
    b_im = b_im.astype(jnp.float32)
    bb_re = cr[..., None] * b_re - ci[..., None] * b_im
    bb_im = cr[..., None] * b_im + ci[..., None] * b_re
    bu_re = jnp.einsum('blgh,gph->blgp', u, bb_re)
    bu_im = jnp.einsum('blgh,gph->blgp', u, bb_im)
    a_re = jnp.broadcast_to(lb_re, (1, L) + lb_re.shape)
    a_im = jnp.broadcast_to(lb_im, (1, L) + lb_im.shape)

    def combine(e1, e2):
        a1r, a1i, x1r, x1i = e1
        a2r, a2i, x2r, x2i = e2
        return (a1r * a2r - a1i * a2i,
                a1r * a2i + a1i * a2r,
                a2r * x1r - a2i * x1i + x2r,
                a2r * x1i + a2i * x1r + x2i)

    _, _, s_re, s_im = lax.associative_scan(combine, (a_re, a_im, bu_re, bu_im), axis=1)
    y = (jnp.einsum('ghp,blgp->blgh', c_re.astype(jnp.float32), s_re)
         - jnp.einsum('ghp,blgp->blgh', c_im.astype(jnp.float32), s_im)
         + d_skip.astype(jnp.float32).reshape(S5_GROUPS, S5_GROUP) * u)
    y = jax.nn.gelu(y.reshape(b_, L, D_MODEL), approximate=False)
    val, gate = jnp.split(y @ w_glu, 2, axis=-1)
    return val * jax.nn.sigmoid(gate)


def swiglu_ffn(h, w_ffn_in, w_ffn_out):
    gate, up = jnp.split(h @ w_ffn_in, 2, axis=-1)
    return (jax.nn.silu(gate) * up) @ w_ffn_out


def setup_inputs(seed: int = 0) -> dict:
    key = jax.random.key(seed)
    ks = jax.random.split(key, 20)
    f32 = jnp.float32
    n = lambda k, shape, s: jax.random.normal(k, shape, f32) * s
    x = jax.random.normal(ks[0], (BATCH, SEQ, D_MODEL), f32)
    norm_gains = 1.0 + n(ks[1], (DEPTH, 4, D_MODEL), 0.02)
    w_in = n(ks[2], (N_EVEN, D_MODEL, IN_WIDTH), D_MODEL ** -0.5)
    w_gate_up = n(ks[3], (N_EVEN, GLA_GATE_RANK, GLA_K_WIDTH), GLA_GATE_RANK ** -0.5)
    b_gate = n(ks[4], (N_EVEN, GLA_K_WIDTH), 0.1)
    gla_norm_gain = 1.0 + n(ks[5], (N_EVEN, GLA_V_WIDTH), 0.02)
    w_out = n(ks[6], (N_EVEN, MIX_WIDTH, D_MODEL), MIX_WIDTH ** -0.5)
    s5_lambda_re = -0.5 * jnp.exp(n(ks[7], (N_ODD, S5_GROUPS, S5_STATE), 0.05))
    s5_lambda_im = (math.pi * jnp.arange(S5_STATE, dtype=f32))[None, None, :] \
        + n(ks[8], (N_ODD, S5_GROUPS, S5_STATE), 0.01)
    s5_log_step = jax.random.uniform(ks[9], (N_ODD, S5_GROUPS), f32,
                                     math.log(S5_DT_MIN), math.log(S5_DT_MAX))
    bs = (2.0 * S5_GROUP) ** -0.5
    s5_b_re = n(ks[10], (N_ODD, S5_GROUPS, S5_STATE, S5_GROUP), bs)
    s5_b_im = n(ks[11], (N_ODD, S5_GROUPS, S5_STATE, S5_GROUP), bs)
    cs = (2.0 * S5_STATE) ** -0.5
    s5_c_re = n(ks[12], (N_ODD, S5_GROUPS, S5_GROUP, S5_STATE), cs)
    s5_c_im = n(ks[13], (N_ODD, S5_GROUPS, S5_GROUP, S5_STATE), cs)
    s5_d = n(ks[14], (N_ODD, D_MODEL), 1.0)
    w_glu = n(ks[15], (N_ODD, D_MODEL, 2 * D_MODEL), D_MODEL ** -0.5)
    w_ffn_in = n(ks[16], (DEPTH, D_MODEL, 2 * D_FF), D_MODEL ** -0.5)
    w_ffn_out = n(ks[17], (DEPTH, D_FF, D_MODEL), D_FF ** -0.5)
    return {"x": x, "norm_gains": norm_gains, "w_in": w_in, "w_gate_up": w_gate_up,
            "b_gate": b_gate, "gla_norm_gain": gla_norm_gain, "w_out": w_out,
            "s5_lambda_re": s5_lambda_re, "s5_lambda_im": s5_lambda_im,
            "s5_log_step": s5_log_step, "s5_b_re": s5_b_re, "s5_b_im": s5_b_im,
            "s5_c_re": s5_c_re, "s5_c_im": s5_c_im, "s5_d": s5_d, "w_glu": w_glu,
            "w_ffn_in": w_ffn_in, "w_ffn_out": w_ffn_out}


def reference(x, norm_gains, w_in, w_gate_up, b_gate, gla_norm_gain, w_out,
              s5_lambda_re, s5_lambda_im, s5_log_step, s5_b_re, s5_b_im,
              s5_c_re, s5_c_im, s5_d, w_glu, w_ffn_in, w_ffn_out):
    h = x.astype(jnp.float32)
    for layer in range(DEPTH):
        g = norm_gains[layer]
        i = layer // 2
        y = rms_norm(h, g[0])
        if layer % 2 == 0:
            y = sb_gla_mixer(y, w_in[i], w_gate_up[i], b_gate[i], gla_norm_gain[i], w_out[i])
        else:
            y = s5_mixer(y, s5_lambda_re[i], s5_lambda_im[i], s5_log_step[i],
                         s5_b_re[i], s5_b_im[i], s5_c_re[i], s5_c_im[i], s5_d[i], w_glu[i])
        h = h + rms_norm(y, g[1])
        y = swiglu_ffn(rms_norm(h, g[2]), w_ffn_in[layer], w_ffn_out[layer])
        h = h + rms_norm(y, g[3])
    return h.astype(x.dtype)
```

```python
import functools
import math

import numpy as np
import jax
import jax.numpy as jnp
from jax import lax
from jax.experimental import pallas as pl
from jax.experimental.pallas import tpu as pltpu

F32 = jnp.float32
BF16 = jnp.bfloat16

NORM_EPS = 1e-6
LANES = 128
SB_HEADS = 8
SB_HEAD_DIM = 128
GLA_HEADS = 8
GLA_DK = 64
GLA_DV = 128
GLA_GATE_RANK = 16
GLA_GATE_TAU = 16.0
S5_GROUP = 16
S5_STATE = 64

SB_TQ = 512
SB_TK = 256
GLA_CHUNK = 128
S5_R = 16
S5_GB = 8

VMEM_LIMIT = 56 * 1024 * 1024

_HIGHEST = lax.Precision.HIGHEST


def _cparams(sem):
    return pltpu.CompilerParams(dimension_semantics=sem, vmem_limit_bytes=VMEM_LIMIT)


def _rms(x, g):
    return x * lax.rsqrt(jnp.mean(x * x, axis=-1, keepdims=True) + NORM_EPS) * g


def _dot(a, b):
    return jnp.dot(a, b, preferred_element_type=F32)


def _dot_nt(a, b):
    return lax.dot_general(a, b, (((1,), (1,)), ((), ())), preferred_element_type=F32)


def _dot_tn(a, b):
    return lax.dot_general(a, b, (((0,), (0,)), ((), ())), preferred_element_type=F32)


def _split_bf16(x):
    hi = x.astype(BF16)
    lo = (x - hi.astype(F32)).astype(BF16)
    return hi, lo


def _norm_kernel(h_ref, g_ref, o_ref):
    o_ref[...] = _rms(h_ref[...], g_ref[...]).astype(o_ref.dtype)


def rms_norm_bf16(h, g, tm=512):
    m, d = h.shape
    return pl.pallas_call(
        _norm_kernel,
        grid=(m // tm,),
        in_specs=[pl.BlockSpec((tm, d), lambda i: (i, 0)),
                  pl.BlockSpec((1, d), lambda i: (0, 0))],
        out_specs=pl.BlockSpec((tm, d), lambda i: (i, 0)),
        out_shape=jax.ShapeDtypeStruct((m, d), BF16),
        compiler_params=_cparams(("parallel",)),
        name="rms_norm",
    )(h, g.reshape(1, d))


def _mm_kernel(a_ref, w_ref, o_ref):
    o_ref[...] = _dot(a_ref[...], w_ref[...]).astype(o_ref.dtype)


def matmul(a, w, out_dtype, tm, tn):
    m, k = a.shape
    n = w.shape[1]
    return pl.pallas_call(
        _mm_kernel,
        grid=(n // tn, m // tm),
        in_specs=[pl.BlockSpec((tm, k), lambda j, i: (i, 0)),
                  pl.BlockSpec((k, tn), lambda j, i: (0, j))],
        out_specs=pl.BlockSpec((tm, tn), lambda j, i: (i, j)),
        out_shape=jax.ShapeDtypeStruct((m, n), out_dtype),
        compiler_params=_cparams(("parallel", "parallel")),
        name="matmul",
    )(a, w)


def _ffn_in_kernel(a_ref, wg_ref, wu_ref, o_ref):
    a = a_ref[...]
    g = _dot(a, wg_ref[...])
    u = _dot(a, wu_ref[...])
    o_ref[...] = (g * jax.nn.sigmoid(g) * u).astype(o_ref.dtype)


def ffn_in(a, w, tm=1024, tn=512):
    m, k = a.shape
    nf = w.shape[1] // 2
    nj = nf // tn
    return pl.pallas_call(
        _ffn_in_kernel,
        grid=(nj, m // tm),
        in_specs=[pl.BlockSpec((tm, k), lambda j, i: (i, 0)),
                  pl.BlockSpec((k, tn), lambda j, i: (0, j)),
                  pl.BlockSpec((k, tn), lambda j, i: (0, j + nj))],
        out_specs=pl.BlockSpec((tm, tn), lambda j, i: (i, j)),
        out_shape=jax.ShapeDtypeStruct((m, nf), BF16),
        compiler_params=_cparams(("parallel", "parallel")),
        name="ffn_in",
    )(a, w, w)


def _mm_res_kernel(a_ref, w_ref, h_ref, gpost_ref, gnext_ref, hout_ref, anext_ref, acc_ref,
                   *, nk, glu, d):
    k = pl.program_id(1)

    @pl.when(k == 0)
    def _():
        acc_ref[...] = jnp.zeros_like(acc_ref)

    acc_ref[...] += _dot(a_ref[...], w_ref[...])

    @pl.when(k == nk - 1)
    def _():
        y = acc_ref[...]
        if glu:
            y = y[:, :d] * jax.nn.sigmoid(y[:, d:])
        hn = h_ref[...] + _rms(y, gpost_ref[...])
        hout_ref[...] = hn
        anext_ref[...] = _rms(hn, gnext_ref[...]).astype(anext_ref.dtype)


def matmul_residual(a, w, h, g_post, g_next, *, glu=False, tm=512, tk=512):
    m, kdim = a.shape
    n = w.shape[1]
    d = h.shape[1]
    nk = kdim // tk
    kern = functools.partial(_mm_res_kernel, nk=nk, glu=glu, d=d)
    return pl.pallas_call(
        kern,
        grid=(m // tm, nk),
        in_specs=[pl.BlockSpec((tm, tk), lambda i, k: (i, k)),
                  pl.BlockSpec((tk, n), lambda i, k: (k, 0)),
                  pl.BlockSpec((tm, d), lambda i, k: (i, 0)),
                  pl.BlockSpec((1, d), lambda i, k: (0, 0)),
                  pl.BlockSpec((1, d), lambda i, k: (0, 0))],
        out_specs=[pl.BlockSpec((tm, d), lambda i, k: (i, 0)),
                   pl.BlockSpec((tm, d), lambda i, k: (i, 0))],
        out_shape=[jax.ShapeDtypeStruct((m, d), F32),
                   jax.ShapeDtypeStruct((m, d), BF16)],
        scratch_shapes=[pltpu.VMEM((tm, n), F32)],
        compiler_params=_cparams(("parallel", "arbitrary")),
        name="matmul_residual",
    )(a, w, h, g_post.reshape(1, d), g_next.reshape(1, d))


def _sb_kernel(q_ref, k_ref, v_ref, u_ref, o_ref, acc_ref, c_ref, *, tq, tk, scale):
    i = pl.program_id(2)
    q = (q_ref[...].astype(F32) * scale).astype(BF16)
    u = u_ref[...]
    acc_ref[...] = jnp.zeros_like(acc_ref)
    c_ref[...] = jnp.zeros_like(c_ref)

    def block(kstart, masked):
        kb = k_ref[pl.ds(kstart, tk), :]
        vb = v_ref[pl.ds(kstart, tk), :]
        z = _dot_nt(q, kb)
        soft = jnp.log1p(jnp.exp(-jnp.abs(z)))
        ls = jnp.minimum(z, 0.0) - soft
        lk = ls - z
        if masked:
            tpos = i * tq + lax.broadcasted_iota(jnp.int32, (tq, tk), 0)
            spos = kstart + lax.broadcasted_iota(jnp.int32, (tq, tk), 1)
            keep = spos < tpos
            lk = jnp.where(keep, lk, 0.0)
        hi, lo = _split_bf16(lk)
        r = _dot(hi, u) + _dot(lo, u)
        w = jnp.exp(ls + r[:, :tk])
        if masked:
            w = jnp.where(keep, w, 0.0)
        pv = _dot(w.astype(BF16), vb)
        acc_ref[...] += jnp.exp(c_ref[...]) * pv
        c_ref[...] += r[:, tk:]

    nstraddle = tq // tk
    for s in range(nstraddle - 1, -1, -1):
        block(pl.multiple_of(i * tq + s * tk, tk), True)

    nfull = i * nstraddle

    def body(n, carry):
        block(pl.multiple_of((nfull - 1 - n) * tk, tk), False)
        return carry

    lax.fori_loop(0, nfull, body, 0)
    o_ref[...] = acc_ref[...].astype(o_ref.dtype)


def _sb_umat(tk):
    j = np.arange(tk)[:, None]
    s = np.arange(tk)[None, :]
    strict = (j > s).astype(np.float32)
    return jnp.asarray(np.concatenate([strict, np.ones((tk, LANES), np.float32)], axis=1), dtype=BF16)


def sb_attention(proj, batch, seq, tq=SB_TQ, tk=SB_TK):
    d = SB_HEAD_DIM
    nq = seq // tq
    kern = functools.partial(_sb_kernel, tq=tq, tk=tk, scale=d ** -0.5)
    return pl.pallas_call(
        kern,
        grid=(batch, SB_HEADS, nq),
        in_specs=[pl.BlockSpec((tq, d), lambda b, h, i: (b * nq + i, h)),
                  pl.BlockSpec((seq, d), lambda b, h, i: (b, SB_HEADS + h)),
                  pl.BlockSpec((seq, d), lambda b, h, i: (b, 2 * SB_HEADS + h)),
                  pl.BlockSpec((tk, tk + LANES), lambda b, h, i: (0, 0))],
        out_specs=pl.BlockSpec((tq, d), lambda b, h, i: (b * nq + i, h)),
        out_shape=jax.ShapeDtypeStruct((batch * seq, SB_HEADS * d), BF16),
        scratch_shapes=[pltpu.VMEM((tq, d), F32), pltpu.VMEM((tq, LANES), F32)],
        compiler_params=_cparams(("parallel", "parallel", "parallel")),
        name="sb_attention",
    )(proj, proj, proj, _sb_umat(tk))


def _gla_levels(c):
    return int(math.log2(c))


def _gla_tmat(c):
    t = np.arange(c)[:, None]
    j = np.arange(c)[None, :]
    mats = [(j <= t), (j > t)]
    for lev in range(_gla_levels(c)):
        half = c >> (lev + 1)
        blk = 2 * half
        bound = (t // blk) * blk + half
        second = (t % blk) >= half
        mats.append(second & (j >= bound) & (j <= t))
        mats.append((~second) & (j > t) & (j < bound))
    return jnp.asarray(np.concatenate(mats, axis=0).astype(np.float32), dtype=BF16)


def _gla_masks(c):
    t = np.arange(c)[:, None]
    s = np.arange(c)[None, :]
    masks = []
    for lev in range(_gla_levels(c)):
        half = c >> (lev + 1)
        blk = 2 * half
        masks.append((t // blk == s // blk) & ((t % blk) >= half) & ((s % blk) < half))
    masks.append(t == s)
    return jnp.asarray(np.concatenate(masks, axis=0).astype(np.float32))


def _gla_kernel(q_ref, k_ref, v_ref, r_ref, glr_ref, wg_ref, bg_ref, gain_ref, t_ref, m_ref,
                o_ref, st_ref, *, c, nlev):
    @pl.when(pl.program_id(2) == 0)
    def _():
        st_ref[...] = jnp.zeros_like(st_ref)

    dk, dv = GLA_DK, GLA_DV
    logits = jnp.dot(glr_ref[...], wg_ref[...], precision=_HIGHEST,
                     preferred_element_type=F32) + bg_ref[...]
    la = jax.nn.log_sigmoid(logits) / GLA_GATE_TAU
    hi, lo = _split_bf16(la)
    tm = t_ref[...]
    f = jnp.exp(_dot(tm, hi) + _dot(tm, lo))

    def fblk(n):
        return f[n * c:(n + 1) * c, :]

    q = q_ref[...].astype(F32) * (dk ** -0.5)
    k = k_ref[...].astype(F32)
    lane = lax.broadcasted_iota(jnp.int32, (c, 2 * dk), 1)
    head_a = lane < dk
    qs = [(q * fblk(2 + 2 * lev)).astype(BF16) for lev in range(nlev)] + [q.astype(BF16)]
    ks = [k * fblk(3 + 2 * lev) for lev in range(nlev)] + [k]
    q_in = q * fblk(0)
    k_out = (k * fblk(1)).astype(BF16)
    decay = f[c - 1:c, :]
    st = st_ref[...]
    st_b = st.astype(BF16)
    new_st = []
    for hd in range(2):
        sel = head_a if hd == 0 else jnp.logical_not(head_a)
        scores = jnp.zeros((c, c), F32)
        for lev in range(nlev + 1):
            kh = jnp.where(sel, ks[lev], 0.0).astype(BF16)
            scores = scores + m_ref[lev * c:(lev + 1) * c, :] * _dot_nt(qs[lev], kh)
        vh = v_ref[:, hd * dv:(hd + 1) * dv]
        qh = jnp.where(sel, q_in, 0.0).astype(BF16)
        o = _dot(scores.astype(BF16), vh) + _dot_nt(qh, st_b)
        o = o * lax.rsqrt(jnp.mean(o * o, axis=-1, keepdims=True) + NORM_EPS)
        rr = r_ref[:, hd * dv:(hd + 1) * dv].astype(F32)
        o = o * gain_ref[:, hd * dv:(hd + 1) * dv] * (rr * jax.nn.sigmoid(rr))
        o_ref[:, hd * dv:(hd + 1) * dv] = o.astype(o_ref.dtype)
        new_st.append(_dot_tn(vh, k_out))
    st_ref[...] = st * decay + jnp.where(lax.broadcasted_iota(jnp.int32, (dv, 2 * dk), 1) < dk,
                                         new_st[0], new_st[1])


def gla_attention(proj, glr, w_gate_pad, b_gate, gain, batch, seq, col0, c=GLA_CHUNK):
    dk, dv = GLA_DK, GLA_DV
    npair = GLA_HEADS // 2
    nc = seq // c
    nlev = _gla_levels(c)
    qb = col0 // (2 * dk)
    kb = qb + npair
    vb = (col0 + 2 * GLA_HEADS * dk) // (2 * dv)
    rb = vb + npair
    kern = functools.partial(_gla_kernel, c=c, nlev=nlev)
    row = lambda b, p, n: b * nc + n
    return pl.pallas_call(
        kern,
        grid=(batch, npair, nc),
        in_specs=[pl.BlockSpec((c, 2 * dk), lambda b, p, n: (row(b, p, n), qb + p)),
                  pl.BlockSpec((c, 2 * dk), lambda b, p, n: (row(b, p, n), kb + p)),
                  pl.BlockSpec((c, 2 * dv), lambda b, p, n: (row(b, p, n), vb + p)),
                  pl.BlockSpec((c, 2 * dv), lambda b, p, n: (row(b, p, n), rb + p)),
                  pl.BlockSpec((c, LANES), lambda b, p, n: (row(b, p, n), 0)),
                  pl.BlockSpec((LANES, 2 * dk), lambda b, p, n: (0, p)),
                  pl.BlockSpec((1, 2 * dk), lambda b, p, n: (0, p)),
                  pl.BlockSpec((1, 2 * dv), lambda b, p, n: (0, p)),
                  pl.BlockSpec(((2 + 2 * nlev) * c, c), lambda b, p, n: (0, 0)),
                  pl.BlockSpec(((nlev + 1) * c, c), lambda b, p, n: (0, 0))],
        out_specs=pl.BlockSpec((c, 2 * dv), lambda b, p, n: (row(b, p, n), p)),
        out_shape=jax.ShapeDtypeStruct((batch * seq, GLA_HEADS * dv), BF16),
        scratch_shapes=[pltpu.VMEM((dv, 2 * dk), F32)],
        compiler_params=_cparams(("parallel", "parallel", "arbitrary")),
        name="gla",
    )(proj, proj, proj, proj, glr, w_gate_pad, b_gate.reshape(1, -1), gain.reshape(1, -1),
      _gla_tmat(c), _gla_masks(c))


def _s5_prep_kernel(lr_ref, li_ref, ls_ref, bre_ref, bim_ref, cre_ref, cim_ref,
                    win_ref, wout_ref, toep_ref, a_ref, bm_ref, *, gb, r):
    h = S5_GROUP
    r16 = r * h
    first = lax.broadcasted_iota(jnp.int32, (1, LANES), 1) < S5_STATE
    kidx = (lax.broadcasted_iota(jnp.int32, ((r + 1) * h, LANES), 0) // h).astype(F32)
    krev = float(r - 1) - kidx[:r16]

    def tile_rows(x):
        return jnp.broadcast_to(x[None], (r, h, LANES)).reshape(r16, LANES)

    for gi in range(gb):
        lr = lr_ref[gi:gi + 1, :]
        li = li_ref[gi:gi + 1, :]
        dt = jnp.exp(ls_ref[gi:gi + 1, :])
        mag1 = jnp.exp(dt * lr)
        ang1 = dt * li
        lbr, lbi = mag1 * jnp.cos(ang1), mag1 * jnp.sin(ang1)
        den = lr * lr + li * li
        nr, ni = lbr - 1.0, lbi
        cr = (nr * lr + ni * li) / den
        ci = (ni * lr - nr * li) / den
        bre, bim = bre_ref[gi], bim_ref[gi]
        bbr = cr * bre - ci * bim
        bbi = cr * bim + ci * bre
        mag = jnp.exp(kidx * (dt * lr))
        ang = kidx * (dt * li)
        pr, pi = mag * jnp.cos(ang), mag * jnp.sin(ang)
        magr = jnp.exp(krev * (dt * lr))
        angr = krev * (dt * li)
        prr, pir = magr * jnp.cos(angr), magr * jnp.sin(angr)
        bbr_t, bbi_t = tile_rows(bbr), tile_rows(bbi)
        re_w = prr * bbr_t - pir * bbi_t
        im_w = prr * bbi_t + pir * bbr_t
        win_ref[gi] = jnp.concatenate(
            [jnp.where(first, re_w, im_w), jnp.where(first, im_w, re_w)], axis=1).astype(win_ref.dtype)
        cr_t, ci_t = tile_rows(cre_ref[gi]), tile_rows(cim_ref[gi])
        pr1, pi1 = pr[h:], pi[h:]
        wout_ref[gi] = jnp.where(first, cr_t * pr1 - ci_t * pi1,
                                 -(cr_t * pi1 + ci_t * pr1)).astype(wout_ref.dtype)
        pr0, pi0 = pr[:r16], pi[:r16]
        wk = jnp.where(first, cr_t * pr0 - ci_t * pi0, -(cr_t * pi0 + ci_t * pr0))
        bb = jnp.where(first, bbr, bbi)
        mt = lax.dot_general(bb, wk, (((1,), (1,)), ((), ())), precision=_HIGHEST,
                             preferred_element_type=F32)
        mt_pad = jnp.concatenate([mt, jnp.zeros_like(mt)], axis=1)
        for i in range(r):
            blk = mt_pad if i == 0 else pltpu.roll(mt_pad, h * i, axis=1)
            toep_ref[gi, h * i:h * (i + 1), :] = blk[:, :r16].astype(toep_ref.dtype)
        lam_r = pr[r * h:r * h + 1, :]
        lam_i = pi[r * h:r * h + 1, :]
        a_ref[gi:gi + 1, :] = lam_r
        bm_ref[gi:gi + 1, :] = jnp.where(first, -lam_i, lam_i)


def s5_prep(lam_re, lam_im, log_step, b_re, b_im, c_re, c_im, r=S5_R, gb=S5_GB):
    g = lam_re.shape[0]
    h = S5_GROUP
    r16 = r * h
    dbl = lambda x: jnp.concatenate([x, x], axis=-1)
    lr2, li2 = dbl(lam_re), dbl(lam_im)
    ls2 = jnp.broadcast_to(log_step[:, None], (g, LANES))
    bre2 = dbl(jnp.swapaxes(b_re, 1, 2))
    bim2 = dbl(jnp.swapaxes(b_im, 1, 2))
    cre2, cim2 = dbl(c_re), dbl(c_im)
    vec = pl.BlockSpec((gb, LANES), lambda i: (i, 0))
    mat = pl.BlockSpec((gb, h, LANES), lambda i: (i, 0, 0))
    kern = functools.partial(_s5_prep_kernel, gb=gb, r=r)
    return pl.pallas_call(
        kern,
        grid=(g // gb,),
        in_specs=[vec, vec, vec, mat, mat, mat, mat],
        out_specs=[pl.BlockSpec((gb, r16, 2 * LANES), lambda i: (i, 0, 0)),
                   pl.BlockSpec((gb, r16, LANES), lambda i: (i, 0, 0)),
                   pl.BlockSpec((gb, r16, r16), lambda i: (i, 0, 0)),
                   vec, vec],
        out_shape=[jax.ShapeDtypeStruct((g, r16, 2 * LANES), BF16),
                   jax.ShapeDtypeStruct((g, r16, LANES), BF16),
                   jax.ShapeDtypeStruct((g, r16, r16), BF16),
                   jax.ShapeDtypeStruct((g, LANES), F32),
                   jax.ShapeDtypeStruct((g, LANES), F32)],
        compiler_params=_cparams(("parallel",)),
        name="s5_prep",
    )(lr2, li2, ls2, bre2, bim2, cre2, cim2)


def _s5_kernel(x_ref, win_ref, wout_ref, toep_ref, a_ref, bm_ref, d_ref, y_ref, s_scr, ss_scr, xp_scr,
               *, gb, batch, jn):
    bj = batch * jn
    for gi in range(gb):
        s2 = _dot(x_ref[gi], win_ref[gi])
        s_scr[gi * bj:(gi + 1) * bj, :] = s2[:, :LANES]
        ss_scr[gi * bj:(gi + 1) * bj, :] = s2[:, LANES:]

    a = a_ref[...]
    bm = bm_ref[...]

    def step(j, carry):
        out = []
        for b in range(batch):
            x, xs = carry[2 * b], carry[2 * b + 1]
            rows = pl.ds(b * jn + j, gb, stride=bj)
            xp_scr[rows, :] = x
            out.append(a * x + bm * xs + s_scr[rows, :])
            out.append(a * xs - bm * x + ss_scr[rows, :])
        return tuple(out)

    zero = jnp.zeros((gb, LANES), F32)
    lax.fori_loop(0, jn, step, (zero,) * (2 * batch))

    for gi in range(gb):
        xg = x_ref[gi]
        xp = xp_scr[gi * bj:(gi + 1) * bj, :].astype(BF16)
        y = _dot_nt(xp, wout_ref[gi]) + _dot(xg, toep_ref[gi]) + xg.astype(F32) * d_ref[gi]
        y_ref[gi] = (0.5 * y * (1.0 + lax.erf(y * (2.0 ** -0.5)))).astype(y_ref.dtype)


def s5_blocks(xb, win, wout, toep, a, bm, d_tile, batch, gb=S5_GB):
    g, bj, r16 = xb.shape
    jn = bj // batch
    kern = functools.partial(_s5_kernel, gb=gb, batch=batch, jn=jn)
    blk3 = lambda s1, s2: pl.BlockSpec((gb, s1, s2), lambda i: (i, 0, 0))
    vec = pl.BlockSpec((gb, LANES), lambda i: (i, 0))
    return pl.pallas_call(
        kern,
        grid=(g // gb,),
        in_specs=[blk3(bj, r16), blk3(r16, 2 * LANES), blk3(r16, LANES), blk3(r16, r16),
                  vec, vec, blk3(1, r16)],
        out_specs=blk3(bj, r16),
        out_shape=jax.ShapeDtypeStruct((g, bj, r16), BF16),
        scratch_shapes=[pltpu.VMEM((gb * bj, LANES), F32)] * 3,
        compiler_params=_cparams(("parallel",)),
        name="s5_blocks",
    )(xb, win, wout, toep, a, bm, d_tile)


def s5_mixer_gelu(a_norm, batch, seq, lam_re, lam_im, log_step, b_re, b_im, c_re, c_im, d_skip, r=S5_R):
    m, d = a_norm.shape
    g = d // S5_GROUP
    jn = seq // r
    win, wout, toep, a, bm = s5_prep(lam_re, lam_im, log_step, b_re, b_im, c_re, c_im, r=r)
    xb = a_norm.reshape(batch, jn, r, g, S5_GROUP).transpose(3, 0, 1, 2, 4).reshape(g, batch * jn, r * S5_GROUP)
    d_tile = jnp.tile(d_skip.reshape(g, 1, S5_GROUP), (1, 1, r))
    yb = s5_blocks(xb, win, wout, toep, a, bm, d_tile, batch)
    return yb.reshape(g, batch, jn, r, S5_GROUP).transpose(1, 2, 3, 0, 4).reshape(m, d)


def kernel(x, norm_gains, w_in, w_gate_up, b_gate, gla_norm_gain, w_out, s5_lambda_re, s5_lambda_im,
           s5_log_step, s5_b_re, s5_b_im, s5_c_re, s5_c_im, s5_d, w_glu, w_ffn_in, w_ffn_out):
    batch, seq, d = x.shape
    m = batch * seq
    depth = norm_gains.shape[0]
    sb_w = SB_HEADS * SB_HEAD_DIM
    main_w = 3 * sb_w + 2 * GLA_HEADS * GLA_DK + 2 * GLA_HEADS * GLA_DV

    h = x.astype(F32).reshape(m, d)
    a = rms_norm_bf16(h, norm_gains[0, 0])
    for layer in range(depth):
        gains = norm_gains[layer]
        i = layer // 2
        if layer % 2 == 0:
            w_main = w_in[i, :, :main_w].astype(BF16)
            w_lr = jnp.pad(w_in[i, :, main_w:], ((0, 0), (0, LANES - GLA_GATE_RANK))).astype(BF16)
            proj = matmul(a, w_main, BF16, tm=1024, tn=1024)
            glr = matmul(a, w_lr, F32, tm=1024, tn=LANES)
            o_sb = sb_attention(proj, batch, seq)
            w_gate_pad = jnp.pad(w_gate_up[i], ((0, LANES - GLA_GATE_RANK), (0, 0)))
            o_gla = gla_attention(proj, glr, w_gate_pad, b_gate[i], gla_norm_gain[i], batch, seq, 3 * sb_w)
            mix = jnp.concatenate([o_sb, o_gla], axis=-1)
            h, a = matmul_residual(mix, w_out[i].astype(BF16), h, gains[1], gains[2])
        else:
            y = s5_mixer_gelu(a, batch, seq, s5_lambda_re[i], s5_lambda_im[i], s5_log_step[i],
                              s5_b_re[i], s5_b_im[i], s5_c_re[i], s5_c_im[i], s5_d[i])
            h, a = matmul_residual(y, w_glu[i].astype(BF16), h, gains[1], gains[2], glu=True)
        f = ffn_in(a, w_ffn_in[layer].astype(BF16))
        g_next = norm_gains[layer + 1, 0] if layer + 1 < depth else gains[3]
        h, a = matmul_residual(f, w_ffn_out[layer].astype(BF16), h, gains[3], g_next)
    return h.reshape(batch, seq, d).astype(x.dtype)
```

```python
import functools
import math

import numpy as np
import jax
import jax.numpy as jnp
from jax import lax
from jax.experimental import pallas as pl
from jax.experimental.pallas import tpu as pltpu

F32 = jnp.float32
BF16 = jnp.bfloat16

NORM_EPS = 1e-6
LANES = 128
SB_HEADS = 8
SB_HEAD_DIM = 128
GLA_HEADS = 8
GLA_DK = 64
GLA_DV = 128
GLA_GATE_RANK = 16
GLA_GATE_TAU = 16.0
S5_GROUP = 16
S5_STATE = 64

SB_TQ = 512
SB_TK = 256
SB_TS = 256
GLA_CHUNK = 128
S5_R = 16
S5_GB = 8

VMEM_LIMIT = 56 * 1024 * 1024

_HIGHEST = lax.Precision.HIGHEST


def _cparams(sem):
    return pltpu.CompilerParams(dimension_semantics=sem, vmem_limit_bytes=VMEM_LIMIT)


def _rms(x, g):
    return x * lax.rsqrt(jnp.mean(x * x, axis=-1, keepdims=True) + NORM_EPS) * g


def _dot(a, b):
    return jnp.dot(a, b, preferred_element_type=F32)


def _dot_nt(a, b):
    return lax.dot_general(a, b, (((1,), (1,)), ((), ())), preferred_element_type=F32)


def _dot_tn(a, b):
    return lax.dot_general(a, b, (((0,), (0,)), ((), ())), preferred_element_type=F32)


def _split_bf16(x):
    hi = x.astype(BF16)
    lo = (x - hi.astype(F32)).astype(BF16)
    return hi, lo


def _norm_kernel(h_ref, g_ref, o_ref):
    o_ref[...] = _rms(h_ref[...], g_ref[...]).astype(o_ref.dtype)


def rms_norm_bf16(h, g, tm=512):
    m, d = h.shape
    return pl.pallas_call(
        _norm_kernel,
        grid=(m // tm,),
        in_specs=[pl.BlockSpec((tm, d), lambda i: (i, 0)),
                  pl.BlockSpec((1, d), lambda i: (0, 0))],
        out_specs=pl.BlockSpec((tm, d), lambda i: (i, 0)),
        out_shape=jax.ShapeDtypeStruct((m, d), BF16),
        compiler_params=_cparams(("parallel",)),
        name="rms_norm",
    )(h, g.reshape(1, d))


def _mm_kernel(a_ref, w_ref, o_ref, wb_ref):
    @pl.when(pl.program_id(1) == 0)
    def _():
        wb_ref[...] = w_ref[...].astype(BF16)

    o_ref[...] = _dot(a_ref[...], wb_ref[...]).astype(o_ref.dtype)


def matmul(a, w, layer, ncols, out_dtype, tm, tn):
    m, k = a.shape
    return pl.pallas_call(
        _mm_kernel,
        grid=(ncols // tn, m // tm),
        in_specs=[pl.BlockSpec((tm, k), lambda j, i: (i, 0)),
                  pl.BlockSpec((None, k, tn), lambda j, i: (layer, 0, j))],
        out_specs=pl.BlockSpec((tm, tn), lambda j, i: (i, j)),
        out_shape=jax.ShapeDtypeStruct((m, ncols), out_dtype),
        scratch_shapes=[pltpu.VMEM((k, tn), BF16)],
        compiler_params=_cparams(("parallel", "arbitrary")),
        name="matmul",
    )(a, w)


def _ffn_in_kernel(a_ref, wg_ref, wu_ref, o_ref, wgb_ref, wub_ref):
    @pl.when(pl.program_id(1) == 0)
    def _():
        wgb_ref[...] = wg_ref[...].astype(BF16)
        wub_ref[...] = wu_ref[...].astype(BF16)

    a = a_ref[...]
    g = _dot(a, wgb_ref[...])
    u = _dot(a, wub_ref[...])
    o_ref[...] = (g * jax.nn.sigmoid(g) * u).astype(o_ref.dtype)


def ffn_in(a, w, layer, tm=1024, tn=512):
    m, k = a.shape
    nf = w.shape[2] // 2
    nj = nf // tn
    return pl.pallas_call(
        _ffn_in_kernel,
        grid=(nj, m // tm),
        in_specs=[pl.BlockSpec((tm, k), lambda j, i: (i, 0)),
                  pl.BlockSpec((None, k, tn), lambda j, i: (layer, 0, j)),
                  pl.BlockSpec((None, k, tn), lambda j, i: (layer, 0, j + nj))],
        out_specs=pl.BlockSpec((tm, tn), lambda j, i: (i, j)),
        out_shape=jax.ShapeDtypeStruct((m, nf), BF16),
        scratch_shapes=[pltpu.VMEM((k, tn), BF16), pltpu.VMEM((k, tn), BF16)],
        compiler_params=_cparams(("parallel", "arbitrary")),
        name="ffn_in",
    )(a, w, w)


def _mm_res_kernel(a_ref, w_ref, h_ref, gpost_ref, gnext_ref, hout_ref, anext_ref, acc_ref,
                   *, nk, glu, d):
    k = pl.program_id(1)

    @pl.when(k == 0)
    def _():
        acc_ref[...] = jnp.zeros_like(acc_ref)

    acc_ref[...] += _dot(a_ref[...], w_ref[...])

    @pl.when(k == nk - 1)
    def _():
        y = acc_ref[...]
        if glu:
            y = y[:, :d] * jax.nn.sigmoid(y[:, d:])
        hn = h_ref[...] + _rms(y, gpost_ref[...])
        hout_ref[...] = hn
        anext_ref[...] = _rms(hn, gnext_ref[...]).astype(anext_ref.dtype)


def matmul_residual(a, w, h, g_post, g_next, *, glu=False, tm=512, tk=512):
    m, kdim = a.shape
    n = w.shape[1]
    d = h.shape[1]
    nk = kdim // tk
    kern = functools.partial(_mm_res_kernel, nk=nk, glu=glu, d=d)
    return pl.pallas_call(
        kern,
        grid=(m // tm, nk),
        in_specs=[pl.BlockSpec((tm, tk), lambda i, k: (i, k)),
                  pl.BlockSpec((tk, n), lambda i, k: (k, 0)),
                  pl.BlockSpec((tm, d), lambda i, k: (i, 0)),
                  pl.BlockSpec((1, d), lambda i, k: (0, 0)),
                  pl.BlockSpec((1, d), lambda i, k: (0, 0))],
        out_specs=[pl.BlockSpec((tm, d), lambda i, k: (i, 0)),
                   pl.BlockSpec((tm, d), lambda i, k: (i, 0))],
        out_shape=[jax.ShapeDtypeStruct((m, d), F32),
                   jax.ShapeDtypeStruct((m, d), BF16)],
        scratch_shapes=[pltpu.VMEM((tm, n), F32)],
        compiler_params=_cparams(("parallel", "arbitrary")),
        name="matmul_residual",
    )(a, w, h, g_post.reshape(1, d), g_next.reshape(1, d))


def _sb_kernel(qt_ref, k_ref, vt_ref, u_ref, o_ref, acc_ref, c_ref, *, tq, tk, ts, scale):
    i = pl.program_id(2)
    qt = (qt_ref[...].astype(F32) * scale).astype(BF16)
    u = u_ref[...]
    acc_ref[...] = jnp.zeros_like(acc_ref)
    c_ref[...] = jnp.zeros_like(c_ref)
    sign = jnp.uint32(0x80000000)
    top16 = jnp.uint32(0xFFFF0000)

    def run(items):
        def keys(it):
            return k_ref[pl.ds(pl.multiple_of(it[0] * tk, tk), tk), :]

        def logits(it):
            return _dot(keys(it), qt[:, it[2]])

        def keep_mask(it, shape):
            spos = it[1] + lax.broadcasted_iota(jnp.int32, shape, 0)
            tpos = it[2].start + lax.broadcasted_iota(jnp.int32, shape, 1)
            return spos < tpos

        def suffix(it, z):
            neg_abs = pltpu.bitcast(pltpu.bitcast(z, jnp.uint32) | sign, F32)
            ls = jnp.minimum(z, 0.0) - jnp.log(1.0 + jnp.exp(neg_abs))
            lk = ls - z
            if it[1] is not None:
                lk = jnp.where(keep_mask(it, z.shape), lk, 0.0)
            hi = pltpu.bitcast(pltpu.bitcast(lk, jnp.uint32) & top16, F32)
            later = _dot(u, hi.astype(BF16)) + _dot(u, (lk - hi).astype(BF16))
            return ls, later, later[0:1, :] + lk[0:1, :]

        def weighted(it, ls, later):
            w = jnp.exp(ls + later)
            if it[1] is not None:
                w = jnp.where(keep_mask(it, w.shape), w, 0.0)
            return _dot(vt_ref[it[0]], w.astype(BF16))

        zs = [logits(it) for it in items]
        sfx = [None] * len(items)
        pvs = [None] * len(items)
        sfx[0] = suffix(items[0], zs[0])
        for n in range(1, len(items)):
            sfx[n] = suffix(items[n], zs[n])
            pvs[n - 1] = weighted(items[n - 1], sfx[n - 1][0], sfx[n - 1][1])
        pvs[-1] = weighted(items[-1], sfx[-1][0], sfx[-1][1])
        for it, (_, _, total), pv in zip(items, sfx, pvs):
            acc_ref[:, it[2]] += jnp.exp(c_ref[:, it[2]]) * pv
            c_ref[:, it[2]] += total

    nstraddle = tq // tk
    items = []
    for s in range(nstraddle - 1, -1, -1):
        for j in range(tq // ts):
            k_lo, q_lo = s * tk, j * ts
            if k_lo >= q_lo + ts - 1:
                continue
            items.append((i * nstraddle + s, k_lo if k_lo + tk > q_lo else None,
                          slice(q_lo, q_lo + ts)))
    run(items)

    def body(n, carry):
        first = i * nstraddle - 1 - 2 * n
        run([(first, None, slice(0, tq)), (first - 1, None, slice(0, tq))])
        return carry

    lax.fori_loop(0, (i * nstraddle) // 2, body, 0)
    o_ref[...] = acc_ref[...].T.astype(o_ref.dtype)


def _sb_umat(tk):
    s = np.arange(tk)[:, None]
    j = np.arange(tk)[None, :]
    return jnp.asarray((j > s).astype(np.float32), dtype=BF16)


def sb_attention(proj, batch, seq, tq=SB_TQ, tk=SB_TK):
    d, nh = SB_HEAD_DIM, SB_HEADS
    assert (tq // tk) % 2 == 0
    nq, nkb = seq // tq, seq // tk
    qt = proj[:, :nh * d].reshape(batch * nq, tq, nh, d).transpose(2, 0, 3, 1)
    vt = proj[:, 2 * nh * d:3 * nh * d].reshape(batch * nkb, tk, nh, d).transpose(2, 0, 3, 1)
    kern = functools.partial(_sb_kernel, tq=tq, tk=tk, ts=SB_TS, scale=d ** -0.5)
    return pl.pallas_call(
        kern,
        grid=(batch, nh, nq),
        in_specs=[pl.BlockSpec((None, None, d, tq), lambda b, h, i: (h, b * nq + i, 0, 0)),
                  pl.BlockSpec((seq, d), lambda b, h, i: (b, nh + h)),
                  pl.BlockSpec((None, nkb, d, tk), lambda b, h, i: (h, b, 0, 0)),
                  pl.BlockSpec((tk, tk), lambda b, h, i: (0, 0))],
        out_specs=pl.BlockSpec((tq, d), lambda b, h, i: (b * nq + i, h)),
        out_shape=jax.ShapeDtypeStruct((batch * seq, nh * d), BF16),
        scratch_shapes=[pltpu.VMEM((d, tq), F32), pltpu.VMEM((1, tq), F32)],
        compiler_params=_cparams(("parallel", "parallel", "parallel")),
        name="sb_attention",
    )(qt, proj, vt, _sb_umat(tk))


def _gla_levels(c):
    return int(math.log2(c))


def _gla_tmat(c):
    t = np.arange(c)[:, None]
    j = np.arange(c)[None, :]
    mats = [(j <= t), (j > t)]
    for lev in range(_gla_levels(c)):
        half = c >> (lev + 1)
        blk = 2 * half
        bound = (t // blk) * blk + half
        second = (t % blk) >= half
        mats.append(second & (j >= bound) & (j <= t))
        mats.append((~second) & (j > t) & (j < bound))
    return jnp.asarray(np.concatenate(mats, axis=0).astype(np.float32), dtype=BF16)


def _gla_masks(c):
    t = np.arange(c)[:, None]
    s = np.arange(c)[None, :]
    masks = []
    for lev in range(_gla_levels(c)):
        half = c >> (lev + 1)
        blk = 2 * half
        masks.append((t // blk == s // blk) & ((t % blk) >= half) & ((s % blk) < half))
    masks.append(t == s)
    return jnp.asarray(np.concatenate(masks, axis=0).astype(np.float32))


def _gla_kernel(q_ref, k_ref, v_ref, r_ref, glr_ref, wg_ref, bg_ref, gain_ref, t_ref, m_ref,
                o_ref, st_ref, *, c, nlev):
    @pl.when(pl.program_id(2) == 0)
    def _():
        st_ref[...] = jnp.zeros_like(st_ref)

    dk, dv = GLA_DK, GLA_DV
    logits = jnp.dot(glr_ref[...], wg_ref[...], precision=_HIGHEST,
                     preferred_element_type=F32) + bg_ref[...]
    la = jax.nn.log_sigmoid(logits) / GLA_GATE_TAU
    hi, lo = _split_bf16(la)
    tm = t_ref[...]
    f = jnp.exp(_dot(tm, hi) + _dot(tm, lo))

    def fblk(n):
        return f[n * c:(n + 1) * c, :]

    q = q_ref[...].astype(F32) * (dk ** -0.5)
    k = k_ref[...].astype(F32)
    lane = lax.broadcasted_iota(jnp.int32, (c, 2 * dk), 1)
    head_a = lane < dk
    qs = [(q * fblk(2 + 2 * lev)).astype(BF16) for lev in range(nlev)] + [q.astype(BF16)]
    ks = [k * fblk(3 + 2 * lev) for lev in range(nlev)] + [k]
    q_in = q * fblk(0)
    k_out = (k * fblk(1)).astype(BF16)
    decay = f[c - 1:c, :]
    st = st_ref[...]
    st_b = st.astype(BF16)
    new_st = []
    for hd in range(2):
        sel = head_a if hd == 0 else jnp.logical_not(head_a)
        scores = jnp.zeros((c, c), F32)
        for lev in range(nlev + 1):
            kh = jnp.where(sel, ks[lev], 0.0).astype(BF16)
            scores = scores + m_ref[lev * c:(lev + 1) * c, :] * _dot_nt(qs[lev], kh)
        vh = v_ref[:, hd * dv:(hd + 1) * dv]
        qh = jnp.where(sel, q_in, 0.0).astype(BF16)
        o = _dot(scores.astype(BF16), vh) + _dot_nt(qh, st_b)
        o = o * lax.rsqrt(jnp.mean(o * o, axis=-1, keepdims=True) + NORM_EPS)
        rr = r_ref[:, hd * dv:(hd + 1) * dv].astype(F32)
        o = o * gain_ref[:, hd * dv:(hd + 1) * dv] * (rr * jax.nn.sigmoid(rr))
        o_ref[:, hd * dv:(hd + 1) * dv] = o.astype(o_ref.dtype)
        new_st.append(_dot_tn(vh, k_out))
    st_ref[...] = st * decay + jnp.where(lax.broadcasted_iota(jnp.int32, (dv, 2 * dk), 1) < dk,
                                         new_st[0], new_st[1])


def gla_attention(proj, glr, w_gate_pad, b_gate, gain, batch, seq, col0, c=GLA_CHUNK):
    dk, dv = GLA_DK, GLA_DV
    npair = GLA_HEADS // 2
    nc = seq // c
    nlev = _gla_levels(c)
    qb = col0 // (2 * dk)
    kb = qb + npair
    vb = (col0 + 2 * GLA_HEADS * dk) // (2 * dv)
    rb = vb + npair
    kern = functools.partial(_gla_kernel, c=c, nlev=nlev)
    row = lambda b, p, n: b * nc + n
    return pl.pallas_call(
        kern,
        grid=(batch, npair, nc),
        in_specs=[pl.BlockSpec((c, 2 * dk), lambda b, p, n: (row(b, p, n), qb + p)),
                  pl.BlockSpec((c, 2 * dk), lambda b, p, n: (row(b, p, n), kb + p)),
                  pl.BlockSpec((c, 2 * dv), lambda b, p, n: (row(b, p, n), vb + p)),
                  pl.BlockSpec((c, 2 * dv), lambda b, p, n: (row(b, p, n), rb + p)),
                  pl.BlockSpec((c, LANES), lambda b, p, n: (row(b, p, n), 0)),
                  pl.BlockSpec((LANES, 2 * dk), lambda b, p, n: (0, p)),
                  pl.BlockSpec((1, 2 * dk), lambda b, p, n: (0, p)),
                  pl.BlockSpec((1, 2 * dv), lambda b, p, n: (0, p)),
                  pl.BlockSpec(((2 + 2 * nlev) * c, c), lambda b, p, n: (0, 0)),
                  pl.BlockSpec(((nlev + 1) * c, c), lambda b, p, n: (0, 0))],
        out_specs=pl.BlockSpec((c, 2 * dv), lambda b, p, n: (row(b, p, n), p)),
        out_shape=jax.ShapeDtypeStruct((batch * seq, GLA_HEADS * dv), BF16),
        scratch_shapes=[pltpu.VMEM((dv, 2 * dk), F32)],
        compiler_params=_cparams(("parallel", "parallel", "arbitrary")),
        name="gla",
    )(proj, proj, proj, proj, glr, w_gate_pad, b_gate.reshape(1, -1), gain.reshape(1, -1),
      _gla_tmat(c), _gla_masks(c))


def _s5_prep_kernel(lr_ref, li_ref, ls_ref, bre_ref, bim_ref, cre_ref, cim_ref,
                    tin_ref, tout_ref, toep_ref, lam_ref, *, gb, r):
    h = S5_GROUP
    r16 = r * h
    first = lax.broadcasted_iota(jnp.int32, (1, LANES), 1) < S5_STATE
    kidx = (lax.broadcasted_iota(jnp.int32, ((r + 1) * h, LANES), 0) // h).astype(F32)
    krev = float(r - 1) - kidx[:r16]
    lane_group = (lax.broadcasted_iota(jnp.int32, (h, r * LANES), 1) % LANES) // h

    def tile_rows(x):
        return jnp.broadcast_to(x[None], (r, h, LANES)).reshape(r16, LANES)

    def by_token(x):
        return x.reshape(r, h, LANES).astype(BF16)

    lam_r, lam_i = [], []
    for gi in range(gb):
        lr = lr_ref[gi:gi + 1, :]
        li = li_ref[gi:gi + 1, :]
        dt = jnp.exp(ls_ref[gi:gi + 1, :])
        mag1 = jnp.exp(dt * lr)
        ang1 = dt * li
        lbr, lbi = mag1 * jnp.cos(ang1), mag1 * jnp.sin(ang1)
        den = lr * lr + li * li
        nr, ni = lbr - 1.0, lbi
        cr = (nr * lr + ni * li) / den
        ci = (ni * lr - nr * li) / den
        bre, bim = bre_ref[gi], bim_ref[gi]
        bbr = cr * bre - ci * bim
        bbi = cr * bim + ci * bre
        mag = jnp.exp(kidx * (dt * lr))
        ang = kidx * (dt * li)
        pr, pi = mag * jnp.cos(ang), mag * jnp.sin(ang)
        magr = jnp.exp(krev * (dt * lr))
        angr = krev * (dt * li)
        prr, pir = magr * jnp.cos(angr), magr * jnp.sin(angr)
        bbr_t, bbi_t = tile_rows(bbr), tile_rows(bbi)
        tin_ref[0, :, gi] = by_token(prr * bbr_t - pir * bbi_t)
        tin_ref[1, :, gi] = by_token(prr * bbi_t + pir * bbr_t)
        cr_t, ci_t = tile_rows(cre_ref[gi]), tile_rows(cim_ref[gi])
        pr1, pi1 = pr[h:], pi[h:]
        tout_ref[0, :, gi] = by_token(cr_t * pr1 - ci_t * pi1)
        tout_ref[1, :, gi] = by_token(-(cr_t * pi1 + ci_t * pr1))
        pr0, pi0 = pr[:r16], pi[:r16]
        wk = jnp.where(first, cr_t * pr0 - ci_t * pi0, -(cr_t * pi0 + ci_t * pr0))
        wk_t = jnp.broadcast_to(wk.reshape(r, 1, h, LANES), (r, gb, h, LANES)).reshape(r * gb * h, LANES)
        bb = jnp.where(first, bbr, bbi)
        mt = lax.dot_general(bb, wk_t, (((1,), (1,)), ((), ())), precision=_HIGHEST,
                             preferred_element_type=F32)
        toep_ref[gi * h:(gi + 1) * h, :] = jnp.where(lane_group == gi, mt, 0.0).astype(toep_ref.dtype)
        lam_r.append(pr[r * h:r * h + 1, :])
        lam_i.append(pi[r * h:r * h + 1, :])

    def pairs(rows):
        return jnp.concatenate([jnp.where(first, rows[2 * q], rows[2 * q + 1]) for q in range(gb // 2)], axis=1)

    lam_ref[0:1, :] = pairs(lam_r)
    lam_ref[1:2, :] = pairs(lam_i)


def s5_prep(lam_re, lam_im, log_step, b_re, b_im, c_re, c_im, r=S5_R, gb=S5_GB):
    g = lam_re.shape[0]
    h = S5_GROUP
    nlb = g // gb
    dbl = lambda x: jnp.concatenate([x, x], axis=-1)
    lr2, li2 = dbl(lam_re), dbl(lam_im)
    ls2 = jnp.broadcast_to(log_step[:, None], (g, LANES))
    bre2 = dbl(jnp.swapaxes(b_re, 1, 2))
    bim2 = dbl(jnp.swapaxes(b_im, 1, 2))
    cre2, cim2 = dbl(c_re), dbl(c_im)
    vec = pl.BlockSpec((gb, LANES), lambda i: (i, 0))
    mat = pl.BlockSpec((gb, h, LANES), lambda i: (i, 0, 0))
    tspec = pl.BlockSpec((None, 2, r, gb, h, LANES), lambda i: (i, 0, 0, 0, 0, 0))
    tshape = jax.ShapeDtypeStruct((nlb, 2, r, gb, h, LANES), BF16)
    kern = functools.partial(_s5_prep_kernel, gb=gb, r=r)
    tin, tout, toep, lam = pl.pallas_call(
        kern,
        grid=(nlb,),
        in_specs=[vec, vec, vec, mat, mat, mat, mat],
        out_specs=[tspec, tspec,
                   pl.BlockSpec((None, gb * h, r * LANES), lambda i: (i, 0, 0)),
                   pl.BlockSpec((None, 2, gb // 2 * LANES), lambda i: (i, 0, 0))],
        out_shape=[tshape, tshape,
                   jax.ShapeDtypeStruct((nlb, gb * h, r * LANES), BF16),
                   jax.ShapeDtypeStruct((nlb, 2, gb // 2 * LANES), F32)],
        compiler_params=_cparams(("parallel",)),
        name="s5_prep",
    )(lr2, li2, ls2, bre2, bim2, cre2, cim2)
    rows = r * gb * h
    return tin.reshape(nlb, 2, rows, LANES), tout.reshape(nlb, 2, rows, LANES), toep, lam


def _s5_pair_mask(r, gb):
    group = (np.arange(r * LANES) % LANES) // S5_GROUP
    lane = np.arange(gb // 2 * LANES)
    target = 2 * (lane // LANES) + (lane % LANES) // S5_STATE
    return jnp.asarray((group[:, None] == target[None, :]).astype(np.float32), dtype=BF16)


def _s5_kernel(*refs, batch, jn, r, gb):
    a_refs = refs[:r]
    tin_ref, tout_ref, toep_ref, lam_ref, mask_ref, d_ref = refs[r:r + 6]
    y_refs = refs[r + 6:2 * r + 6]
    bdin_scr, bdout_scr, bdt_scr, s_scr, xp_scr = refs[2 * r + 6:]
    npair = gb // 2
    half = npair * LANES

    for q in range(npair):
        mq = mask_ref[:, q * LANES:(q + 1) * LANES]
        for part in range(2):
            cols = slice((part * npair + q) * LANES, (part * npair + q + 1) * LANES)
            bdin_scr[:, cols] = tin_ref[part] * mq
            bdout_scr[:, cols] = tout_ref[part] * mq
    for i in range(r):
        if i > 0:
            bdt_scr[i * LANES:(i + 1) * LANES, :i * LANES] = jnp.zeros((LANES, i * LANES), bdt_scr.dtype)
        bdt_scr[i * LANES:(i + 1) * LANES, i * LANES:] = toep_ref[:, :(r - i) * LANES]

    a = jnp.concatenate([ar[...] for ar in a_refs], axis=1)
    s_scr[...] = _dot(a, bdin_scr[...])
    lam_r = lam_ref[0:1, :]
    lam_i = lam_ref[1:2, :]

    def step(j, carry):
        out = []
        for b in range(batch):
            xr, xi = carry[2 * b], carry[2 * b + 1]
            row = pl.ds(b * jn + j, 1)
            xp_scr[row, :half] = xr
            xp_scr[row, half:] = xi
            s = s_scr[row, :]
            out.append(lam_r * xr - lam_i * xi + s[:, :half])
            out.append(lam_r * xi + lam_i * xr + s[:, half:])
        return tuple(out)

    zero = jnp.zeros((1, half), F32)
    lax.fori_loop(0, jn, step, (zero,) * (2 * batch))

    y = _dot_nt(xp_scr[...].astype(BF16), bdout_scr[...]) + _dot(a, bdt_scr[...])
    d = d_ref[...]
    for i in range(r):
        yi = y[:, i * LANES:(i + 1) * LANES] + a_refs[i][...].astype(F32) * d
        y_refs[i][...] = (0.5 * yi * (1.0 + lax.erf(yi * (2.0 ** -0.5)))).astype(y_refs[i].dtype)


def s5_mixer_gelu(a_norm, batch, seq, lam_re, lam_im, log_step, b_re, b_im, c_re, c_im, d_skip,
                  r=S5_R, gb=S5_GB):
    m, d = a_norm.shape
    assert gb * S5_GROUP == LANES
    nlb = d // LANES
    jn = seq // r
    bj = batch * jn
    half = gb // 2 * LANES
    tin, tout, toep, lam = s5_prep(lam_re, lam_im, log_step, b_re, b_im, c_re, c_im, r=r, gb=gb)
    a_blocks = a_norm.reshape(bj, r * d)
    tok = lambda i: pl.BlockSpec((bj, LANES), lambda lb: (0, i * nlb + lb))
    whole = lambda *shape: pl.BlockSpec((None,) + shape, lambda lb: (lb,) + (0,) * len(shape))
    kern = functools.partial(_s5_kernel, batch=batch, jn=jn, r=r, gb=gb)
    ys = pl.pallas_call(
        kern,
        grid=(nlb,),
        in_specs=[tok(i) for i in range(r)] + [
            whole(2, r * LANES, LANES), whole(2, r * LANES, LANES), whole(LANES, r * LANES), whole(2, half),
            pl.BlockSpec((r * LANES, half), lambda lb: (0, 0)),
            pl.BlockSpec((1, LANES), lambda lb: (0, lb))],
        out_specs=[pl.BlockSpec((bj, LANES), lambda lb: (0, lb)) for _ in range(r)],
        out_shape=[jax.ShapeDtypeStruct((bj, d), BF16) for _ in range(r)],
        scratch_shapes=[pltpu.VMEM((r * LANES, 2 * half), BF16), pltpu.VMEM((r * LANES, 2 * half), BF16),
                        pltpu.VMEM((r * LANES, r * LANES), BF16),
                        pltpu.VMEM((bj, 2 * half), F32), pltpu.VMEM((bj, 2 * half), F32)],
        compiler_params=_cparams(("parallel",)),
        name="s5_blocks",
    )(*([a_blocks] * r), tin, tout, toep, lam, _s5_pair_mask(r, gb), d_skip.reshape(1, d))
    return jnp.stack(ys, axis=1).reshape(m, d)


def kernel(x, norm_gains, w_in, w_gate_up, b_gate, gla_norm_gain, w_out, s5_lambda_re, s5_lambda_im,
           s5_log_step, s5_b_re, s5_b_im, s5_c_re, s5_c_im, s5_d, w_glu, w_ffn_in, w_ffn_out):
    batch, seq, d = x.shape
    m = batch * seq
    depth = norm_gains.shape[0]
    sb_w = SB_HEADS * SB_HEAD_DIM
    main_w = 3 * sb_w + 2 * GLA_HEADS * GLA_DK + 2 * GLA_HEADS * GLA_DV

    h = x.astype(F32).reshape(m, d)
    a = rms_norm_bf16(h, norm_gains[0, 0])
    for layer in range(depth):
        gains = norm_gains[layer]
        i = layer // 2
        if layer % 2 == 0:
            w_lr = jnp.pad(w_in[i, :, main_w:], ((0, 0), (0, LANES - GLA_GATE_RANK)))[None]
            proj = matmul(a, w_in, i, main_w, BF16, tm=1024, tn=1024)
            glr = matmul(a, w_lr, 0, LANES, F32, tm=1024, tn=LANES)
            o_sb = sb_attention(proj, batch, seq)
            w_gate_pad = jnp.pad(w_gate_up[i], ((0, LANES - GLA_GATE_RANK), (0, 0)))
            o_gla = gla_attention(proj, glr, w_gate_pad, b_gate[i], gla_norm_gain[i], batch, seq, 3 * sb_w)
            mix = jnp.concatenate([o_sb, o_gla], axis=-1)
            h, a = matmul_residual(mix, w_out[i].astype(BF16), h, gains[1], gains[2], tk=d)
        else:
            y = s5_mixer_gelu(a, batch, seq, s5_lambda_re[i], s5_lambda_im[i], s5_log_step[i],
                              s5_b_re[i], s5_b_im[i], s5_c_re[i], s5_c_im[i], s5_d[i])
            h, a = matmul_residual(y, w_glu[i].astype(BF16), h, gains[1], gains[2], glu=True, tk=d // 2)
        f = ffn_in(a, w_ffn_in, layer)
        g_next = norm_gains[layer + 1, 0] if layer + 1 < depth else gains[3]
        h, a = matmul_residual(f, w_ffn_out[layer].astype(BF16), h, gains[3], g_next,
                               tk=w_ffn_out.shape[1] // 4)
    return h.reshape(batch, seq, d).astype(x.dtype)
```

```python
import functools
import math

import numpy as np
import jax
import jax.numpy as jnp
from jax import lax
from jax.experimental import pallas as pl
from jax.experimental.pallas import tpu as pltpu

F32 = jnp.float32
BF16 = jnp.bfloat16

NORM_EPS = 1e-6
LANES = 128
SB_HEADS = 8
SB_HEAD_DIM = 128
GLA_HEADS = 8
GLA_DK = 64
GLA_DV = 128
GLA_GATE_RANK = 16
GLA_GATE_TAU = 16.0
S5_GROUP = 16
S5_STATE = 64

SB_TQ = 512
SB_TK = 256
SB_TS = 256
GLA_CHUNK = 128
S5_R = 16
S5_GB = 8

VMEM_LIMIT = 56 * 1024 * 1024

_HIGHEST = lax.Precision.HIGHEST
LOG2E = math.log2(math.e)


def _cparams(sem):
    return pltpu.CompilerParams(dimension_semantics=sem, vmem_limit_bytes=VMEM_LIMIT)


def _rms(x, g):
    return x * lax.rsqrt(jnp.mean(x * x, axis=-1, keepdims=True) + NORM_EPS) * g


def _dot(a, b):
    return jnp.dot(a, b, preferred_element_type=F32)


def _dot_nt(a, b):
    return lax.dot_general(a, b, (((1,), (1,)), ((), ())), preferred_element_type=F32)


def _dot_tn(a, b):
    return lax.dot_general(a, b, (((0,), (0,)), ((), ())), preferred_element_type=F32)


def _split_bf16(x):
    hi = x.astype(BF16)
    lo = (x - hi.astype(F32)).astype(BF16)
    return hi, lo


def _norm_kernel(h_ref, g_ref, o_ref):
    o_ref[...] = _rms(h_ref[...], g_ref[...]).astype(o_ref.dtype)


def rms_norm_bf16(h, g, tm=512):
    m, d = h.shape
    return pl.pallas_call(
        _norm_kernel,
        grid=(m // tm,),
        in_specs=[pl.BlockSpec((tm, d), lambda i: (i, 0)),
                  pl.BlockSpec((1, d), lambda i: (0, 0))],
        out_specs=pl.BlockSpec((tm, d), lambda i: (i, 0)),
        out_shape=jax.ShapeDtypeStruct((m, d), BF16),
        compiler_params=_cparams(("parallel",)),
        name="rms_norm",
    )(h, g.reshape(1, d))


def _mm_kernel(a_ref, w_ref, o_ref, wb_ref):
    @pl.when(pl.program_id(1) == 0)
    def _():
        wb_ref[...] = w_ref[...].astype(BF16)

    o_ref[...] = _dot(a_ref[...], wb_ref[...]).astype(o_ref.dtype)


def matmul(a, w, layer, ncols, out_dtype, tm, tn):
    m, k = a.shape
    return pl.pallas_call(
        _mm_kernel,
        grid=(ncols // tn, m // tm),
        in_specs=[pl.BlockSpec((tm, k), lambda j, i: (i, 0)),
                  pl.BlockSpec((None, k, tn), lambda j, i: (layer, 0, j))],
        out_specs=pl.BlockSpec((tm, tn), lambda j, i: (i, j)),
        out_shape=jax.ShapeDtypeStruct((m, ncols), out_dtype),
        scratch_shapes=[pltpu.VMEM((k, tn), BF16)],
        compiler_params=_cparams(("parallel", "arbitrary")),
        name="matmul",
    )(a, w)


def _ffn_in_kernel(a_ref, wg_ref, wu_ref, o_ref, wgb_ref, wub_ref):
    @pl.when(pl.program_id(1) == 0)
    def _():
        wgb_ref[...] = wg_ref[...].astype(BF16)
        wub_ref[...] = wu_ref[...].astype(BF16)

    a = a_ref[...]
    g = _dot(a, wgb_ref[...])
    u = _dot(a, wub_ref[...])
    o_ref[...] = (g * jax.nn.sigmoid(g) * u).astype(o_ref.dtype)


def ffn_in(a, w, layer, tm=1024, tn=512):
    m, k = a.shape
    nf = w.shape[2] // 2
    nj = nf // tn
    return pl.pallas_call(
        _ffn_in_kernel,
        grid=(nj, m // tm),
        in_specs=[pl.BlockSpec((tm, k), lambda j, i: (i, 0)),
                  pl.BlockSpec((None, k, tn), lambda j, i: (layer, 0, j)),
                  pl.BlockSpec((None, k, tn), lambda j, i: (layer, 0, j + nj))],
        out_specs=pl.BlockSpec((tm, tn), lambda j, i: (i, j)),
        out_shape=jax.ShapeDtypeStruct((m, nf), BF16),
        scratch_shapes=[pltpu.VMEM((k, tn), BF16), pltpu.VMEM((k, tn), BF16)],
        compiler_params=_cparams(("parallel", "arbitrary")),
        name="ffn_in",
    )(a, w, w)


def _mm_res_kernel(*refs, nk, glu, d, n_a):
    a_refs = refs[:n_a]
    w_ref, h_ref, gpost_ref, gnext_ref, hout_ref, anext_ref, acc_ref = refs[n_a:]
    k = pl.program_id(1)

    @pl.when(k == 0)
    def _():
        acc_ref[...] = jnp.zeros_like(acc_ref)

    if n_a == 1:
        acc_ref[...] += _dot(a_refs[0][...].astype(BF16), w_ref[...])
    else:
        for j in range(n_a):
            @pl.when(k == j)
            def _(j=j):
                acc_ref[...] += _dot(a_refs[j][...].astype(BF16), w_ref[...])

    @pl.when(k == nk - 1)
    def _():
        y = acc_ref[...]
        if glu:
            y = y[:, :d] * jax.nn.sigmoid(y[:, d:])
        hn = h_ref[...] + _rms(y, gpost_ref[...])
        hout_ref[...] = hn
        anext_ref[...] = _rms(hn, gnext_ref[...]).astype(anext_ref.dtype)


def matmul_residual(a, w, h, g_post, g_next, *, glu=False, tm=512, tk=512):
    a_list = list(a) if isinstance(a, (tuple, list)) else [a]
    n_a = len(a_list)
    m = a_list[0].shape[0]
    n = w.shape[1]
    d = h.shape[1]
    if n_a == 1:
        nk = a_list[0].shape[1] // tk
        a_specs = [pl.BlockSpec((tm, tk), lambda i, k: (i, k))]
    else:
        tk = a_list[0].shape[1]
        assert all(x.shape[1] == tk for x in a_list)
        nk = n_a
        a_specs = [pl.BlockSpec((tm, tk), lambda i, k: (i, 0)) for _ in a_list]
    kern = functools.partial(_mm_res_kernel, nk=nk, glu=glu, d=d, n_a=n_a)
    return pl.pallas_call(
        kern,
        grid=(m // tm, nk),
        in_specs=a_specs + [
                  pl.BlockSpec((tk, n), lambda i, k: (k, 0)),
                  pl.BlockSpec((tm, d), lambda i, k: (i, 0), pipeline_mode=pl.Buffered(1)),
                  pl.BlockSpec((1, d), lambda i, k: (0, 0)),
                  pl.BlockSpec((1, d), lambda i, k: (0, 0))],
        out_specs=[pl.BlockSpec((tm, d), lambda i, k: (i, 0)),
                   pl.BlockSpec((tm, d), lambda i, k: (i, 0))],
        out_shape=[jax.ShapeDtypeStruct((m, d), F32),
                   jax.ShapeDtypeStruct((m, d), BF16)],
        scratch_shapes=[pltpu.VMEM((tm, n), F32)],
        compiler_params=_cparams(("parallel", "arbitrary")),
        name="matmul_residual",
    )(*a_list, w, h, g_post.reshape(1, d), g_next.reshape(1, d))


def _sb_kernel(qt_ref, k_ref, vt_ref, u_ref, o_ref, acc_ref, c_ref, *, tq, tk, ts, scale):
    i = pl.program_id(2)
    qt = (qt_ref[...].astype(F32) * scale).astype(BF16)
    u = u_ref[...]
    acc_ref[...] = jnp.zeros_like(acc_ref)
    c_ref[...] = jnp.zeros_like(c_ref)

    def run(items):
        def keys(it):
            return k_ref[pl.ds(pl.multiple_of(it[0] * tk, tk), tk), :]

        def logits(it):
            return _dot(keys(it), qt[:, it[2]])

        def keep_mask(it, shape):
            spos = it[1] + lax.broadcasted_iota(jnp.int32, shape, 0)
            tpos = it[2].start + lax.broadcasted_iota(jnp.int32, shape, 1)
            return spos < tpos

        def suffix(it, z):
            ls = jnp.minimum(z, 0.0) - jnp.log(1.0 + jnp.exp2(jnp.abs(z) * (-LOG2E)))
            lk = ls - z
            if it[1] is not None:
                lk = jnp.where(keep_mask(it, z.shape), lk, 0.0)
            hi, lo = _split_bf16(lk)
            later = _dot(u, hi) + _dot(u, lo)
            return ls, later, later[0:1, :] + lk[0:1, :]

        def weighted(it, ls, later):
            w = jnp.exp(ls + later)
            if it[1] is not None:
                w = jnp.where(keep_mask(it, w.shape), w, 0.0)
            return _dot(vt_ref[it[0]], w.astype(BF16))

        zs = [logits(it) for it in items]
        sfx = [None] * len(items)
        pvs = [None] * len(items)
        sfx[0] = suffix(items[0], zs[0])
        for n in range(1, len(items)):
            sfx[n] = suffix(items[n], zs[n])
            pvs[n - 1] = weighted(items[n - 1], sfx[n - 1][0], sfx[n - 1][1])
        pvs[-1] = weighted(items[-1], sfx[-1][0], sfx[-1][1])
        for it, (_, _, total), pv in zip(items, sfx, pvs):
            acc_ref[:, it[2]] += jnp.exp(c_ref[:, it[2]]) * pv
            c_ref[:, it[2]] += total

    nstraddle = tq // tk
    items = []
    for s in range(nstraddle - 1, -1, -1):
        for j in range(tq // ts):
            k_lo, q_lo = s * tk, j * ts
            if k_lo >= q_lo + ts - 1:
                continue
            items.append((i * nstraddle + s, k_lo if k_lo + tk > q_lo else None,
                          slice(q_lo, q_lo + ts)))
    run(items)

    def body(n, carry):
        first = i * nstraddle - 1 - 2 * n
        run([(first, None, slice(0, tq)), (first - 1, None, slice(0, tq))])
        return carry

    lax.fori_loop(0, (i * nstraddle) // 2, body, 0)
    o_ref[...] = acc_ref[...].T.astype(o_ref.dtype)


def _sb_umat(tk):
    s = np.arange(tk)[:, None]
    j = np.arange(tk)[None, :]
    return jnp.asarray((j > s).astype(np.float32), dtype=BF16)


def sb_attention(proj, batch, seq, tq=SB_TQ, tk=SB_TK):
    d, nh = SB_HEAD_DIM, SB_HEADS
    assert (tq // tk) % 2 == 0
    nq, nkb = seq // tq, seq // tk
    qt = proj[:, :nh * d].reshape(batch * nq, tq, nh, d).transpose(2, 0, 3, 1)
    vt = proj[:, 2 * nh * d:3 * nh * d].reshape(batch * nkb, tk, nh, d).transpose(2, 0, 3, 1)
    kern = functools.partial(_sb_kernel, tq=tq, tk=tk, ts=SB_TS, scale=d ** -0.5)
    return pl.pallas_call(
        kern,
        grid=(batch, nh, nq),
        in_specs=[pl.BlockSpec((None, None, d, tq), lambda b, h, i: (h, b * nq + i, 0, 0)),
                  pl.BlockSpec((seq, d), lambda b, h, i: (b, nh + h)),
                  pl.BlockSpec((None, nkb, d, tk), lambda b, h, i: (h, b, 0, 0)),
                  pl.BlockSpec((tk, tk), lambda b, h, i: (0, 0))],
        out_specs=pl.BlockSpec((tq, d), lambda b, h, i: (b * nq + i, h)),
        out_shape=jax.ShapeDtypeStruct((batch * seq, nh * d), BF16),
        scratch_shapes=[pltpu.VMEM((d, tq), F32), pltpu.VMEM((1, tq), F32)],
        compiler_params=_cparams(("parallel", "parallel", "parallel")),
        name="sb_attention",
    )(qt, proj, vt, _sb_umat(tk))


def _gla_levels(c):
    return int(math.log2(c))


def _gla_masks(c):
    t = np.arange(c)[:, None]
    s = np.arange(c)[None, :]
    masks = []
    for lev in range(_gla_levels(c)):
        half = c >> (lev + 1)
        blk = 2 * half
        masks.append((t // blk == s // blk) & ((t % blk) >= half) & ((s % blk) < half))
    masks.append(t == s)
    m = np.concatenate(masks, axis=0).astype(np.float32)
    return jnp.asarray(np.concatenate([m, m], axis=1))


def _gla_boundary_rows(cum, half):
    c, width = cum.shape
    blk = 2 * half
    sub = 8
    if half >= sub:
        return jnp.concatenate(
            [jnp.broadcast_to(cum[b * blk + half - 1:b * blk + half, :], (blk, width)) for b in range(c // blk)],
            axis=0)
    x = cum.reshape(c // sub, sub, width)
    row = lax.broadcasted_iota(jnp.int32, (c // sub, sub, width), 1)
    out = None
    for b in range(sub // blk):
        piece = jnp.broadcast_to(x[:, b * blk + half - 1:b * blk + half, :], x.shape)
        out = piece if out is None else jnp.where(row >= b * blk, piece, out)
    return out.reshape(c, width)


def _gla_kernel(q_ref, k_ref, v_ref, r_ref, glr_ref, wg_ref, bg_ref, gain_ref, t_ref, m_ref, bm_ref,
                o_ref, st_ref, *, c, nlev):
    @pl.when(pl.program_id(1) == 0)
    def _():
        st_ref[...] = jnp.zeros_like(st_ref)

    dk, dv = GLA_DK, GLA_DV
    npair = GLA_HEADS // 2
    logits = jnp.dot(glr_ref[...], wg_ref[...], precision=_HIGHEST,
                     preferred_element_type=F32) + bg_ref[...]
    la = jax.nn.log_sigmoid(logits) / GLA_GATE_TAU
    hi, lo = _split_bf16(la)
    cum = _dot(t_ref[...], hi) + _dot(t_ref[...], lo)
    total = cum[c - 1:c, :]
    f_in = jnp.exp(cum)
    f_out = jnp.exp(total - cum)
    f_lev = [jnp.exp(-jnp.abs(cum - _gla_boundary_rows(cum, c >> (lev + 1)))) for lev in range(nlev)]

    q = q_ref[...].astype(F32) * (dk ** -0.5)
    k = k_ref[...].astype(F32)
    qs = [(q * f).astype(BF16) for f in f_lev] + [q.astype(BF16)]
    ks = [(k * f).astype(BF16) for f in f_lev] + [k.astype(BF16)]
    q_in = (q * f_in).astype(BF16)
    k_out = (k * f_out).astype(BF16)
    ones = jnp.ones((c, 2 * dv), BF16)
    head_a = lax.broadcasted_iota(jnp.int32, (c, 2 * dk), 1) < dk
    zero_k = jnp.zeros((c, 2 * dk), BF16)
    zero_v = jnp.zeros((c, dv), BF16)

    for p in range(npair):
        lanes = slice(p * 2 * dk, (p + 1) * 2 * dk)
        wide = slice(p * 2 * dv, (p + 1) * 2 * dv)
        scores = jnp.zeros((c, 2 * c), F32)
        for lev in range(nlev + 1):
            kp = ks[lev][:, lanes]
            kstack = jnp.concatenate([jnp.where(head_a, kp, zero_k), jnp.where(head_a, zero_k, kp)], axis=0)
            scores = scores + m_ref[lev * c:(lev + 1) * c, :] * _dot_nt(qs[lev][:, lanes], kstack)
        v = v_ref[:, wide]
        v_bd = jnp.concatenate([jnp.concatenate([v[:, :dv], zero_v], axis=1),
                                jnp.concatenate([zero_v, v[:, dv:]], axis=1)], axis=0)
        st = st_ref[p]
        o = _dot(scores.astype(BF16), v_bd) + _dot(q_in[:, lanes], st.astype(BF16))
        tot_col = _dot_tn(hi[:, lanes], ones) + _dot_tn(lo[:, lanes], ones)
        st_ref[p] = jnp.exp(tot_col) * st + bm_ref[...] * _dot_tn(k_out[:, lanes], v)
        for hd in range(2):
            oh = o[:, hd * dv:(hd + 1) * dv]
            oh = oh * lax.rsqrt(jnp.mean(oh * oh, axis=-1, keepdims=True) + NORM_EPS)
            cols = slice(p * 2 * dv + hd * dv, p * 2 * dv + (hd + 1) * dv)
            rr = r_ref[:, cols].astype(F32)
            o_ref[:, cols] = (oh * gain_ref[:, cols] * (rr * jax.nn.sigmoid(rr))).astype(o_ref.dtype)


def gla_attention(proj, glr, w_gate_pad, b_gate, gain, batch, seq, col0, c=GLA_CHUNK):
    dk, dv, nh = GLA_DK, GLA_DV, GLA_HEADS
    nc = seq // c
    nlev = _gla_levels(c)
    kw, vw = nh * dk, nh * dv
    qb = col0 // kw
    vb = (col0 + 2 * kw) // vw
    tincl = jnp.asarray(np.tril(np.ones((c, c), np.float32)), dtype=BF16)
    pair_blocks = jnp.asarray(np.kron(np.eye(2, dtype=np.float32), np.ones((dk, dv), np.float32)))
    kern = functools.partial(_gla_kernel, c=c, nlev=nlev)
    row = lambda b, n: b * nc + n
    const = lambda shape: pl.BlockSpec(shape, lambda b, n: (0, 0))
    return pl.pallas_call(
        kern,
        grid=(batch, nc),
        in_specs=[pl.BlockSpec((c, kw), lambda b, n: (row(b, n), qb)),
                  pl.BlockSpec((c, kw), lambda b, n: (row(b, n), qb + 1)),
                  pl.BlockSpec((c, vw), lambda b, n: (row(b, n), vb)),
                  pl.BlockSpec((c, vw), lambda b, n: (row(b, n), vb + 1)),
                  pl.BlockSpec((c, LANES), lambda b, n: (row(b, n), 0)),
                  const((LANES, kw)), const((1, kw)), const((1, vw)),
                  const((c, c)), const(((nlev + 1) * c, 2 * c)), const((2 * dk, 2 * dv))],
        out_specs=pl.BlockSpec((c, vw), lambda b, n: (row(b, n), 0)),
        out_shape=jax.ShapeDtypeStruct((batch * seq, vw), BF16),
        scratch_shapes=[pltpu.VMEM((nh // 2, 2 * dk, 2 * dv), F32)],
        compiler_params=_cparams(("parallel", "arbitrary")),
        name="gla",
    )(proj, proj, proj, proj, glr, w_gate_pad, b_gate.reshape(1, -1), gain.reshape(1, -1),
      tincl, _gla_masks(c), pair_blocks)


def _s5_prep_kernel(lr_ref, li_ref, ls_ref, bre_ref, bim_ref, cre_ref, cim_ref,
                    tin_ref, tout_ref, toep_ref, lam_ref, *, gb, r):
    h = S5_GROUP
    r16 = r * h
    first = lax.broadcasted_iota(jnp.int32, (1, LANES), 1) < S5_STATE
    npow = -(-(r + 1) // 8) * 8
    kidx = lax.broadcasted_iota(jnp.int32, (npow, LANES), 0).astype(F32)
    lane_group = (lax.broadcasted_iota(jnp.int32, (h, r * LANES), 1) % LANES) // h

    def tile_rows(x):
        return jnp.broadcast_to(x[None], (r, h, LANES)).reshape(r16, LANES)

    def by_token(x):
        return x.reshape(r, h, LANES).astype(BF16)

    def rows_of(table, powers):
        return jnp.concatenate([jnp.broadcast_to(table[k:k + 1, :], (h, LANES)) for k in powers], axis=0)

    lam_r, lam_i = [], []
    for gi in range(gb):
        lr = lr_ref[gi:gi + 1, :]
        li = li_ref[gi:gi + 1, :]
        dt = jnp.exp(ls_ref[gi:gi + 1, :])
        mag = jnp.exp(kidx * (dt * lr))
        ang = kidx * (dt * li)
        pw_r, pw_i = mag * jnp.cos(ang), mag * jnp.sin(ang)
        lbr, lbi = pw_r[1:2, :], pw_i[1:2, :]
        den = lr * lr + li * li
        nr, ni = lbr - 1.0, lbi
        cr = (nr * lr + ni * li) / den
        ci = (ni * lr - nr * li) / den
        bre, bim = bre_ref[gi], bim_ref[gi]
        bbr = cr * bre - ci * bim
        bbi = cr * bim + ci * bre
        pr, pi = rows_of(pw_r, range(r + 1)), rows_of(pw_i, range(r + 1))
        prr, pir = rows_of(pw_r, range(r - 1, -1, -1)), rows_of(pw_i, range(r - 1, -1, -1))
        bbr_t, bbi_t = tile_rows(bbr), tile_rows(bbi)
        tin_ref[0, :, gi] = by_token(prr * bbr_t - pir * bbi_t)
        tin_ref[1, :, gi] = by_token(prr * bbi_t + pir * bbr_t)
        cr_t, ci_t = tile_rows(cre_ref[gi]), tile_rows(cim_ref[gi])
        pr1, pi1 = pr[h:], pi[h:]
        tout_ref[0, :, gi] = by_token(cr_t * pr1 - ci_t * pi1)
        tout_ref[1, :, gi] = by_token(-(cr_t * pi1 + ci_t * pr1))
        pr0, pi0 = pr[:r16], pi[:r16]
        wk = jnp.where(first, cr_t * pr0 - ci_t * pi0, -(cr_t * pi0 + ci_t * pr0))
        wk_t = jnp.broadcast_to(wk.reshape(r, 1, h, LANES), (r, gb, h, LANES)).reshape(r * gb * h, LANES)
        bb = jnp.where(first, bbr, bbi)
        mt = _dot_nt(bb.astype(BF16), wk_t.astype(BF16))
        toep_ref[gi * h:(gi + 1) * h, :] = jnp.where(lane_group == gi, mt, 0.0).astype(toep_ref.dtype)
        lam_r.append(pr[r * h:r * h + 1, :])
        lam_i.append(pi[r * h:r * h + 1, :])

    def pairs(rows):
        return jnp.concatenate([jnp.where(first, rows[2 * q], rows[2 * q + 1]) for q in range(gb // 2)], axis=1)

    lam_ref[0:1, :] = pairs(lam_r)
    lam_ref[1:2, :] = pairs(lam_i)


def s5_prep(lam_re, lam_im, log_step, b_re, b_im, c_re, c_im, r=S5_R, gb=S5_GB):
    g = lam_re.shape[0]
    h = S5_GROUP
    nlb = g // gb
    dbl = lambda x: jnp.concatenate([x, x], axis=-1)
    lr2, li2 = dbl(lam_re), dbl(lam_im)
    ls2 = jnp.broadcast_to(log_step[:, None], (g, LANES))
    bre2 = dbl(jnp.swapaxes(b_re, 1, 2))
    bim2 = dbl(jnp.swapaxes(b_im, 1, 2))
    cre2, cim2 = dbl(c_re), dbl(c_im)
    vec = pl.BlockSpec((gb, LANES), lambda i: (i, 0))
    mat = pl.BlockSpec((gb, h, LANES), lambda i: (i, 0, 0))
    tspec = pl.BlockSpec((None, 2, r, gb, h, LANES), lambda i: (i, 0, 0, 0, 0, 0))
    tshape = jax.ShapeDtypeStruct((nlb, 2, r, gb, h, LANES), BF16)
    kern = functools.partial(_s5_prep_kernel, gb=gb, r=r)
    tin, tout, toep, lam = pl.pallas_call(
        kern,
        grid=(nlb,),
        in_specs=[vec, vec, vec, mat, mat, mat, mat],
        out_specs=[tspec, tspec,
                   pl.BlockSpec((None, gb * h, r * LANES), lambda i: (i, 0, 0)),
                   pl.BlockSpec((None, 2, gb // 2 * LANES), lambda i: (i, 0, 0))],
        out_shape=[tshape, tshape,
                   jax.ShapeDtypeStruct((nlb, gb * h, r * LANES), BF16),
                   jax.ShapeDtypeStruct((nlb, 2, gb // 2 * LANES), F32)],
        compiler_params=_cparams(("parallel",)),
        name="s5_prep",
    )(lr2, li2, ls2, bre2, bim2, cre2, cim2)
    rows = r * gb * h
    return tin.reshape(nlb, 2, rows, LANES), tout.reshape(nlb, 2, rows, LANES), toep, lam


def _s5_pair_mask(r, gb):
    group = (np.arange(r * LANES) % LANES) // S5_GROUP
    lane = np.arange(gb // 2 * LANES)
    target = 2 * (lane // LANES) + (lane % LANES) // S5_STATE
    return jnp.asarray((group[:, None] == target[None, :]).astype(np.float32), dtype=BF16)


def _s5_kernel(a_ref, tin_ref, tout_ref, toep_ref, lam_ref, mask_ref, d_ref, y_ref,
               a32_scr, bdin_scr, bdout_scr, bdt_scr, s_scr, xp_scr, *, batch, jn, r, gb):
    bj = batch * jn
    npair = gb // 2
    half = npair * LANES
    a32_scr[...] = a_ref[...].astype(F32)
    a_tok = [a32_scr[pl.ds(i, bj, stride=r), :] for i in range(r)]

    for q in range(npair):
        mq = mask_ref[:, q * LANES:(q + 1) * LANES]
        for part in range(2):
            cols = slice((part * npair + q) * LANES, (part * npair + q + 1) * LANES)
            bdin_scr[:, cols] = tin_ref[part] * mq
            bdout_scr[:, cols] = tout_ref[part] * mq
    for i in range(r):
        if i > 0:
            bdt_scr[i * LANES:(i + 1) * LANES, :i * LANES] = jnp.zeros((LANES, i * LANES), bdt_scr.dtype)
        bdt_scr[i * LANES:(i + 1) * LANES, i * LANES:] = toep_ref[:, :(r - i) * LANES]

    a = jnp.concatenate([x.astype(BF16) for x in a_tok], axis=1)
    s_scr[...] = _dot(a, bdin_scr[...])
    lam_r = lam_ref[0:1, :]
    lam_i = lam_ref[1:2, :]

    def step(j, carry):
        out = []
        for b in range(batch):
            xr, xi = carry[2 * b], carry[2 * b + 1]
            row = pl.ds(b * jn + j, 1)
            xp_scr[row, :half] = xr
            xp_scr[row, half:] = xi
            s = s_scr[row, :]
            out.append(lam_r * xr - lam_i * xi + s[:, :half])
            out.append(lam_r * xi + lam_i * xr + s[:, half:])
        return tuple(out)

    zero = jnp.zeros((1, half), F32)
    lax.fori_loop(0, jn, step, (zero,) * (2 * batch))

    y = _dot_nt(xp_scr[...].astype(BF16), bdout_scr[...]) + _dot(a, bdt_scr[...])
    d = d_ref[...]
    for i in range(r):
        yi = y[:, i * LANES:(i + 1) * LANES] + a_tok[i] * d
        y_ref[pl.ds(i, bj, stride=r), :] = 0.5 * yi * (1.0 + lax.erf(yi * (2.0 ** -0.5)))


def s5_mixer_gelu(a_norm, batch, seq, lam_re, lam_im, log_step, b_re, b_im, c_re, c_im, d_skip,
                  r=S5_R, gb=S5_GB):
    m, d = a_norm.shape
    assert gb * S5_GROUP == LANES
    nlb = d // LANES
    jn = seq // r
    bj = batch * jn
    half = gb // 2 * LANES
    tin, tout, toep, lam = s5_prep(lam_re, lam_im, log_step, b_re, b_im, c_re, c_im, r=r, gb=gb)
    lane_block = pl.BlockSpec((m, LANES), lambda lb: (0, lb))
    whole = lambda *shape: pl.BlockSpec((None,) + shape, lambda lb: (lb,) + (0,) * len(shape))
    kern = functools.partial(_s5_kernel, batch=batch, jn=jn, r=r, gb=gb)
    return pl.pallas_call(
        kern,
        grid=(nlb,),
        in_specs=[lane_block,
                  whole(2, r * LANES, LANES), whole(2, r * LANES, LANES), whole(LANES, r * LANES), whole(2, half),
                  pl.BlockSpec((r * LANES, half), lambda lb: (0, 0)),
                  pl.BlockSpec((1, LANES), lambda lb: (0, lb))],
        out_specs=lane_block,
        out_shape=jax.ShapeDtypeStruct((m, d), F32),
        scratch_shapes=[pltpu.VMEM((m, LANES), F32),
                        pltpu.VMEM((r * LANES, 2 * half), BF16), pltpu.VMEM((r * LANES, 2 * half), BF16),
                        pltpu.VMEM((r * LANES, r * LANES), BF16),
                        pltpu.VMEM((bj, 2 * half), F32), pltpu.VMEM((bj, 2 * half), F32)],
        compiler_params=_cparams(("parallel",)),
        name="s5_blocks",
    )(a_norm, tin, tout, toep, lam, _s5_pair_mask(r, gb), d_skip.reshape(1, d))


def kernel(x, norm_gains, w_in, w_gate_up, b_gate, gla_norm_gain, w_out, s5_lambda_re, s5_lambda_im,
           s5_log_step, s5_b_re, s5_b_im, s5_c_re, s5_c_im, s5_d, w_glu, w_ffn_in, w_ffn_out):
    batch, seq, d = x.shape
    m = batch * seq
    depth = norm_gains.shape[0]
    sb_w = SB_HEADS * SB_HEAD_DIM
    main_w = 3 * sb_w + 2 * GLA_HEADS * GLA_DK + 2 * GLA_HEADS * GLA_DV

    h = x.astype(F32).reshape(m, d)
    a = rms_norm_bf16(h, norm_gains[0, 0])
    for layer in range(depth):
        gains = norm_gains[layer]
        i = layer // 2
        if layer % 2 == 0:
            w_lr = jnp.pad(w_in[i, :, main_w:], ((0, 0), (0, LANES - GLA_GATE_RANK)))[None]
            proj = matmul(a, w_in, i, main_w, BF16, tm=1024, tn=1024)
            glr = matmul(a, w_lr, 0, LANES, F32, tm=1024, tn=LANES)
            o_sb = sb_attention(proj, batch, seq)
            w_gate_pad = jnp.pad(w_gate_up[i], ((0, LANES - GLA_GATE_RANK), (0, 0)))
            o_gla = gla_attention(proj, glr, w_gate_pad, b_gate[i], gla_norm_gain[i], batch, seq, 3 * sb_w)
            h, a = matmul_residual((o_sb, o_gla), w_out[i].astype(BF16), h, gains[1], gains[2])
        else:
            y = s5_mixer_gelu(a, batch, seq, s5_lambda_re[i], s5_lambda_im[i], s5_log_step[i],
                              s5_b_re[i], s5_b_im[i], s5_c_re[i], s5_c_im[i], s5_d[i])
            h, a = matmul_residual(y, w_glu[i].astype(BF16), h, gains[1], gains[2], glu=True, tk=d // 2)
        f = ffn_in(a, w_ffn_in, layer)
        g_next = norm_gains[layer + 1, 0] if layer + 1 < depth else gains[3]
        h, a = matmul_residual(f, w_ffn_out[layer].astype(BF16), h, gains[3], g_next, tm=1024, tk=512)
    return h.reshape(batch, seq, d).astype(x.dtype)
```

```python
import functools
import math

import numpy as np
import jax
import jax.numpy as jnp
from jax import lax
from jax.experimental import pallas as pl
from jax.experimental.pallas import tpu as pltpu

F32 = jnp.float32
BF16 = jnp.bfloat16

NORM_EPS = 1e-6
LANES = 128
SB_HEADS = 8
SB_HEAD_DIM = 128
GLA_HEADS = 8
GLA_DK = 64
GLA_DV = 128
GLA_GATE_RANK = 16
GLA_GATE_TAU = 16.0
S5_GROUP = 16
S5_STATE = 64

SB_TQ = 512
SB_TK = 256
SB_TS = 256
GLA_CHUNK = 128
S5_R = 16
S5_GB = 8
S5_TOEP_SPLIT = 4

VMEM_LIMIT = 56 * 1024 * 1024

_HIGHEST = lax.Precision.HIGHEST
LOG2E = math.log2(math.e)


def _cparams(sem):
    return pltpu.CompilerParams(dimension_semantics=sem, vmem_limit_bytes=VMEM_LIMIT)


def _rms(x, g):
    return x * lax.rsqrt(jnp.mean(x * x, axis=-1, keepdims=True) + NORM_EPS) * g


def _dot(a, b):
    return jnp.dot(a, b, preferred_element_type=F32)


def _dot_nt(a, b):
    return lax.dot_general(a, b, (((1,), (1,)), ((), ())), preferred_element_type=F32)


def _dot_tn(a, b):
    return lax.dot_general(a, b, (((0,), (0,)), ((), ())), preferred_element_type=F32)


def _split_bf16(x):
    hi = x.astype(BF16)
    lo = (x - hi.astype(F32)).astype(BF16)
    return hi, lo


def _norm_kernel(h_ref, g_ref, o_ref):
    o_ref[...] = _rms(h_ref[...], g_ref[...]).astype(o_ref.dtype)


def rms_norm_bf16(h, g, tm=512):
    m, d = h.shape
    return pl.pallas_call(
        _norm_kernel,
        grid=(m // tm,),
        in_specs=[pl.BlockSpec((tm, d), lambda i: (i, 0)),
                  pl.BlockSpec((1, d), lambda i: (0, 0))],
        out_specs=pl.BlockSpec((tm, d), lambda i: (i, 0)),
        out_shape=jax.ShapeDtypeStruct((m, d), BF16),
        compiler_params=_cparams(("parallel",)),
        name="rms_norm",
    )(h, g.reshape(1, d))


def _mm_kernel(a_ref, w_ref, o_ref, wb_ref):
    @pl.when(pl.program_id(1) == 0)
    def _():
        wb_ref[...] = w_ref[...].astype(BF16)

    o_ref[...] = _dot(a_ref[...], wb_ref[...]).astype(o_ref.dtype)


def matmul(a, w, layer, ncols, out_dtype, tm, tn):
    m, k = a.shape
    return pl.pallas_call(
        _mm_kernel,
        grid=(ncols // tn, m // tm),
        in_specs=[pl.BlockSpec((tm, k), lambda j, i: (i, 0)),
                  pl.BlockSpec((None, k, tn), lambda j, i: (layer, 0, j))],
        out_specs=pl.BlockSpec((tm, tn), lambda j, i: (i, j)),
        out_shape=jax.ShapeDtypeStruct((m, ncols), out_dtype),
        scratch_shapes=[pltpu.VMEM((k, tn), BF16)],
        compiler_params=_cparams(("parallel", "arbitrary")),
        name="matmul",
    )(a, w)


def _ffn_in_kernel(a_ref, wg_ref, wu_ref, o_ref, wgb_ref, wub_ref):
    @pl.when(pl.program_id(1) == 0)
    def _():
        wgb_ref[...] = wg_ref[...].astype(BF16)
        wub_ref[...] = wu_ref[...].astype(BF16)

    a = a_ref[...]
    g = _dot(a, wgb_ref[...])
    u = _dot(a, wub_ref[...])
    o_ref[...] = (g * jax.nn.sigmoid(g) * u).astype(o_ref.dtype)


def ffn_in(a, w, layer, tm=1024, tn=512):
    m, k = a.shape
    nf = w.shape[2] // 2
    nj = nf // tn
    return pl.pallas_call(
        _ffn_in_kernel,
        grid=(nj, m // tm),
        in_specs=[pl.BlockSpec((tm, k), lambda j, i: (i, 0)),
                  pl.BlockSpec((None, k, tn), lambda j, i: (layer, 0, j)),
                  pl.BlockSpec((None, k, tn), lambda j, i: (layer, 0, j + nj))],
        out_specs=pl.BlockSpec((tm, tn), lambda j, i: (i, j)),
        out_shape=jax.ShapeDtypeStruct((m, nf), BF16),
        scratch_shapes=[pltpu.VMEM((k, tn), BF16), pltpu.VMEM((k, tn), BF16)],
        compiler_params=_cparams(("parallel", "arbitrary")),
        name="ffn_in",
    )(a, w, w)


def _mm_res_kernel(*refs, nk, glu, d, n_a):
    a_refs = refs[:n_a]
    w_ref, h_ref, gpost_ref, gnext_ref, hout_ref, anext_ref, acc_ref = refs[n_a:]
    k = pl.program_id(1)

    @pl.when(k == 0)
    def _():
        acc_ref[...] = jnp.zeros_like(acc_ref)

    if n_a == 1:
        acc_ref[...] += _dot(a_refs[0][...], w_ref[...])
    else:
        width = a_refs[0].shape[1]
        for j in range(n_a):
            acc_ref[...] += _dot(a_refs[j][...], w_ref[j * width:(j + 1) * width, :])

    @pl.when(k == nk - 1)
    def _():
        y = acc_ref[...]
        if glu:
            y = y[:, :d] * jax.nn.sigmoid(y[:, d:])
        hn = h_ref[...] + _rms(y, gpost_ref[...])
        hout_ref[...] = hn
        anext_ref[...] = _rms(hn, gnext_ref[...]).astype(anext_ref.dtype)


def matmul_residual(a, w, layer, h, g_post, g_next, *, glu=False, tm=512, tk=512):
    a_list = list(a) if isinstance(a, (tuple, list)) else [a]
    n_a = len(a_list)
    m = a_list[0].shape[0]
    kdim, n = w.shape[1:]
    d = h.shape[1]
    if n_a == 1:
        a_specs = [pl.BlockSpec((tm, tk), lambda i, k: (i, k))]
    else:
        tk = kdim
        assert sum(x.shape[1] for x in a_list) == kdim
        a_specs = [pl.BlockSpec((tm, x.shape[1]), lambda i, k: (i, 0)) for x in a_list]
    nk = kdim // tk
    kern = functools.partial(_mm_res_kernel, nk=nk, glu=glu, d=d, n_a=n_a)
    return pl.pallas_call(
        kern,
        grid=(m // tm, nk),
        in_specs=a_specs + [
                  pl.BlockSpec((None, tk, n), lambda i, k: (layer, k, 0)),
                  pl.BlockSpec((tm, d), lambda i, k: (i, 0)),
                  pl.BlockSpec((1, d), lambda i, k: (0, 0)),
                  pl.BlockSpec((1, d), lambda i, k: (0, 0))],
        out_specs=[pl.BlockSpec((tm, d), lambda i, k: (i, 0)),
                   pl.BlockSpec((tm, d), lambda i, k: (i, 0))],
        out_shape=[jax.ShapeDtypeStruct((m, d), F32),
                   jax.ShapeDtypeStruct((m, d), BF16)],
        scratch_shapes=[pltpu.VMEM((tm, n), F32)],
        compiler_params=_cparams(("parallel", "arbitrary")),
        name="matmul_residual",
    )(*a_list, w, h, g_post.reshape(1, d), g_next.reshape(1, d))


def _sb_kernel(qt_ref, k_ref, vt_ref, u_ref, o_ref, acc_ref, c_ref, *, tq, tk, ts, scale):
    i = pl.program_id(2)
    qt = (qt_ref[...].astype(F32) * scale).astype(BF16)
    u = u_ref[...]
    acc_ref[...] = jnp.zeros_like(acc_ref)
    c_ref[...] = jnp.zeros_like(c_ref)

    def run(items):
        def keys(it):
            return k_ref[pl.ds(pl.multiple_of(it[0] * tk, tk), tk), :]

        def logits(it):
            return _dot(keys(it), qt[:, it[2]])

        def keep_mask(it, shape):
            spos = it[1] + lax.broadcasted_iota(jnp.int32, shape, 0)
            tpos = it[2].start + lax.broadcasted_iota(jnp.int32, shape, 1)
            return spos < tpos

        def suffix(it, z):
            ls = jnp.minimum(z, 0.0) - jnp.log(1.0 + jnp.exp2(jnp.abs(z) * (-LOG2E)))
            lk = ls - z
            if it[1] is not None:
                lk = jnp.where(keep_mask(it, z.shape), lk, 0.0)
            later = _dot(u, lk.astype(BF16))
            return ls, later, later[0:1, :] + lk[0:1, :]

        def weighted(it, ls, later):
            w = jnp.exp(ls + later)
            if it[1] is not None:
                w = jnp.where(keep_mask(it, w.shape), w, 0.0)
            return _dot(vt_ref[it[0]], w.astype(BF16))

        zs = [logits(it) for it in items]
        sfx = [None] * len(items)
        pvs = [None] * len(items)
        sfx[0] = suffix(items[0], zs[0])
        for n in range(1, len(items)):
            sfx[n] = suffix(items[n], zs[n])
            pvs[n - 1] = weighted(items[n - 1], sfx[n - 1][0], sfx[n - 1][1])
        pvs[-1] = weighted(items[-1], sfx[-1][0], sfx[-1][1])
        for it, (_, _, total), pv in zip(items, sfx, pvs):
            acc_ref[:, it[2]] += jnp.exp(c_ref[:, it[2]]) * pv
            c_ref[:, it[2]] += total

    nstraddle = tq // tk
    items = []
    for s in range(nstraddle - 1, -1, -1):
        for j in range(tq // ts):
            k_lo, q_lo = s * tk, j * ts
            if k_lo >= q_lo + ts - 1:
                continue
            items.append((i * nstraddle + s, k_lo if k_lo + tk > q_lo else None,
                          slice(q_lo, q_lo + ts)))
    run(items)

    def full_blocks(first, count):
        run([(first - n, None, slice(0, tq)) for n in range(count)])

    nfull = i * nstraddle
    group = 2 * nstraddle

    @pl.when(nfull % group != 0)
    def _():
        full_blocks(nfull - 1, nstraddle)

    def body(n, carry):
        full_blocks(nfull - nfull % group - 1 - group * n, group)
        return carry

    lax.fori_loop(0, nfull // group, body, 0)
    o_ref[...] = acc_ref[...].T.astype(o_ref.dtype)


def _sb_umat(tk):
    s = np.arange(tk)[:, None]
    j = np.arange(tk)[None, :]
    return jnp.asarray((j > s).astype(np.float32), dtype=BF16)


def sb_attention(proj, batch, seq, tq=SB_TQ, tk=SB_TK):
    d, nh = SB_HEAD_DIM, SB_HEADS
    assert (tq // tk) % 2 == 0
    nq, nkb = seq // tq, seq // tk
    qt = proj[:, :nh * d].reshape(batch * nq, tq, nh, d).transpose(2, 0, 3, 1)
    vt = proj[:, 2 * nh * d:3 * nh * d].reshape(batch * nkb, tk, nh, d).transpose(2, 0, 3, 1)
    kern = functools.partial(_sb_kernel, tq=tq, tk=tk, ts=SB_TS, scale=d ** -0.5)
    return pl.pallas_call(
        kern,
        grid=(batch, nh, nq),
        in_specs=[pl.BlockSpec((None, None, d, tq), lambda b, h, i: (h, b * nq + i, 0, 0)),
                  pl.BlockSpec((seq, d), lambda b, h, i: (b, nh + h)),
                  pl.BlockSpec((None, nkb, d, tk), lambda b, h, i: (h, b, 0, 0)),
                  pl.BlockSpec((tk, tk), lambda b, h, i: (0, 0))],
        out_specs=pl.BlockSpec((tq, d), lambda b, h, i: (b * nq + i, h)),
        out_shape=jax.ShapeDtypeStruct((batch * seq, nh * d), BF16),
        scratch_shapes=[pltpu.VMEM((d, tq), F32), pltpu.VMEM((1, tq), F32)],
        compiler_params=_cparams(("parallel", "parallel", "parallel")),
        name="sb_attention",
    )(qt, proj, vt, _sb_umat(tk))


def _gla_levels(c):
    return int(math.log2(c))


def _gla_masks(c):
    t = np.arange(c)[:, None]
    s = np.arange(c)[None, :]
    masks = []
    for lev in range(_gla_levels(c)):
        half = c >> (lev + 1)
        blk = 2 * half
        masks.append((t // blk == s // blk) & ((t % blk) >= half) & ((s % blk) < half))
    masks.append(t == s)
    m = np.concatenate(masks, axis=0).astype(np.float32)
    return jnp.asarray(np.concatenate([m, m], axis=1))


def _gla_boundary_rows(cum, half):
    c, width = cum.shape
    blk = 2 * half
    sub = 8
    if half >= sub:
        return jnp.concatenate(
            [jnp.broadcast_to(cum[b * blk + half - 1:b * blk + half, :], (blk, width)) for b in range(c // blk)],
            axis=0)
    x = cum.reshape(c // sub, sub, width)
    row = lax.broadcasted_iota(jnp.int32, (c // sub, sub, width), 1)
    out = None
    for b in range(sub // blk):
        piece = jnp.broadcast_to(x[:, b * blk + half - 1:b * blk + half, :], x.shape)
        out = piece if out is None else jnp.where(row >= b * blk, piece, out)
    return out.reshape(c, width)


def _gla_kernel(q_ref, k_ref, v_ref, r_ref, glr_ref, wg_ref, bg_ref, gain_ref, t_ref, m_ref, bm_ref,
                o_ref, st_ref, *, c, nlev):
    @pl.when(pl.program_id(1) == 0)
    def _():
        st_ref[...] = jnp.zeros_like(st_ref)

    dk, dv = GLA_DK, GLA_DV
    npair = GLA_HEADS // 2
    logits = jnp.dot(glr_ref[...], wg_ref[...], precision=_HIGHEST,
                     preferred_element_type=F32) + bg_ref[...]
    la = jax.nn.log_sigmoid(logits) / GLA_GATE_TAU
    hi, lo = _split_bf16(la)
    cum = _dot(t_ref[...], hi) + _dot(t_ref[...], lo)
    total = cum[c - 1:c, :]
    f_in = jnp.exp(cum)
    f_out = jnp.exp(total - cum)
    f_lev = [jnp.exp(-jnp.abs(cum - _gla_boundary_rows(cum, c >> (lev + 1)))) for lev in range(nlev)]

    q = q_ref[...].astype(F32) * (dk ** -0.5)
    k = k_ref[...].astype(F32)
    qs = [(q * f).astype(BF16) for f in f_lev] + [q.astype(BF16)]
    ks = [(k * f).astype(BF16) for f in f_lev] + [k.astype(BF16)]
    q_in = (q * f_in).astype(BF16)
    k_out = (k * f_out).astype(BF16)
    ones = jnp.ones((c, 2 * dv), BF16)
    head_a = lax.broadcasted_iota(jnp.int32, (c, 2 * dk), 1) < dk
    zero_k = jnp.zeros((c, 2 * dk), BF16)
    zero_v = jnp.zeros((c, dv), BF16)

    for p in range(npair):
        lanes = slice(p * 2 * dk, (p + 1) * 2 * dk)
        wide = slice(p * 2 * dv, (p + 1) * 2 * dv)
        scores = jnp.zeros((c, 2 * c), F32)
        for lev in range(nlev + 1):
            kp = ks[lev][:, lanes]
            kstack = jnp.concatenate([jnp.where(head_a, kp, zero_k), jnp.where(head_a, zero_k, kp)], axis=0)
            scores = scores + m_ref[lev * c:(lev + 1) * c, :] * _dot_nt(qs[lev][:, lanes], kstack)
        v = v_ref[:, wide]
        v_bd = jnp.concatenate([jnp.concatenate([v[:, :dv], zero_v], axis=1),
                                jnp.concatenate([zero_v, v[:, dv:]], axis=1)], axis=0)
        st = st_ref[p]
        o = _dot(scores.astype(BF16), v_bd) + _dot(q_in[:, lanes], st.astype(BF16))
        tot_col = _dot_tn(hi[:, lanes], ones) + _dot_tn(lo[:, lanes], ones)
        st_ref[p] = jnp.exp(tot_col) * st + bm_ref[...] * _dot_tn(k_out[:, lanes], v)
        for hd in range(2):
            oh = o[:, hd * dv:(hd + 1) * dv]
            oh = oh * lax.rsqrt(jnp.mean(oh * oh, axis=-1, keepdims=True) + NORM_EPS)
            cols = slice(p * 2 * dv + hd * dv, p * 2 * dv + (hd + 1) * dv)
            rr = r_ref[:, cols].astype(F32)
            o_ref[:, cols] = (oh * gain_ref[:, cols] * (rr * jax.nn.sigmoid(rr))).astype(o_ref.dtype)


def gla_attention(proj, glr, w_gate_pad, b_gate, gain, batch, seq, col0, c=GLA_CHUNK):
    dk, dv, nh = GLA_DK, GLA_DV, GLA_HEADS
    nc = seq // c
    nlev = _gla_levels(c)
    kw, vw = nh * dk, nh * dv
    qb = col0 // kw
    vb = (col0 + 2 * kw) // vw
    tincl = jnp.asarray(np.tril(np.ones((c, c), np.float32)), dtype=BF16)
    pair_blocks = jnp.asarray(np.kron(np.eye(2, dtype=np.float32), np.ones((dk, dv), np.float32)))
    kern = functools.partial(_gla_kernel, c=c, nlev=nlev)
    row = lambda b, n: b * nc + n
    const = lambda shape: pl.BlockSpec(shape, lambda b, n: (0, 0))
    return pl.pallas_call(
        kern,
        grid=(batch, nc),
        in_specs=[pl.BlockSpec((c, kw), lambda b, n: (row(b, n), qb)),
                  pl.BlockSpec((c, kw), lambda b, n: (row(b, n), qb + 1)),
                  pl.BlockSpec((c, vw), lambda b, n: (row(b, n), vb)),
                  pl.BlockSpec((c, vw), lambda b, n: (row(b, n), vb + 1)),
                  pl.BlockSpec((c, LANES), lambda b, n: (row(b, n), 0)),
                  const((LANES, kw)), const((1, kw)), const((1, vw)),
                  const((c, c)), const(((nlev + 1) * c, 2 * c)), const((2 * dk, 2 * dv))],
        out_specs=pl.BlockSpec((c, vw), lambda b, n: (row(b, n), 0)),
        out_shape=jax.ShapeDtypeStruct((batch * seq, vw), BF16),
        scratch_shapes=[pltpu.VMEM((nh // 2, 2 * dk, 2 * dv), F32)],
        compiler_params=_cparams(("parallel", "arbitrary")),
        name="gla",
    )(proj, proj, proj, proj, glr, w_gate_pad, b_gate.reshape(1, -1), gain.reshape(1, -1),
      tincl, _gla_masks(c), pair_blocks)


def _s5_prep_kernel(lr_ref, li_ref, ls_ref, bre_ref, bim_ref, cre_ref, cim_ref,
                    tin_ref, tout_ref, toep_ref, lam_ref, *, gb, r):
    h = S5_GROUP
    r16 = r * h
    first = lax.broadcasted_iota(jnp.int32, (1, LANES), 1) < S5_STATE
    npow = -(-(r + 1) // 8) * 8
    kidx = lax.broadcasted_iota(jnp.int32, (npow, LANES), 0).astype(F32)
    lane_group = (lax.broadcasted_iota(jnp.int32, (h, r * LANES), 1) % LANES) // h

    def tile_rows(x):
        return jnp.broadcast_to(x[None], (r, h, LANES)).reshape(r16, LANES)

    def by_token(x):
        return x.reshape(r, h, LANES).astype(BF16)

    def rows_of(table, powers):
        return jnp.concatenate([jnp.broadcast_to(table[k:k + 1, :], (h, LANES)) for k in powers], axis=0)

    lam_r, lam_i = [], []
    for gi in range(gb):
        lr = lr_ref[gi:gi + 1, :]
        li = li_ref[gi:gi + 1, :]
        dt = jnp.exp(ls_ref[gi:gi + 1, :])
        mag = jnp.exp(kidx * (dt * lr))
        ang = kidx * (dt * li)
        pw_r, pw_i = mag * jnp.cos(ang), mag * jnp.sin(ang)
        lbr, lbi = pw_r[1:2, :], pw_i[1:2, :]
        den = lr * lr + li * li
        nr, ni = lbr - 1.0, lbi
        cr = (nr * lr + ni * li) / den
        ci = (ni * lr - nr * li) / den
        bre, bim = bre_ref[gi], bim_ref[gi]
        bbr = cr * bre - ci * bim
        bbi = cr * bim + ci * bre
        pr, pi = rows_of(pw_r, range(r + 1)), rows_of(pw_i, range(r + 1))
        prr, pir = rows_of(pw_r, range(r - 1, -1, -1)), rows_of(pw_i, range(r - 1, -1, -1))
        bbr_t, bbi_t = tile_rows(bbr), tile_rows(bbi)
        tin_ref[0, :, gi] = by_token(prr * bbr_t - pir * bbi_t)
        tin_ref[1, :, gi] = by_token(prr * bbi_t + pir * bbr_t)
        cr_t, ci_t = tile_rows(cre_ref[gi]), tile_rows(cim_ref[gi])
        pr1, pi1 = pr[h:], pi[h:]
        tout_ref[0, :, gi] = by_token(cr_t * pr1 - ci_t * pi1)
        tout_ref[1, :, gi] = by_token(-(cr_t * pi1 + ci_t * pr1))
        pr0, pi0 = pr[:r16], pi[:r16]
        wk = jnp.where(first, cr_t * pr0 - ci_t * pi0, -(cr_t * pi0 + ci_t * pr0))
        wk_t = jnp.broadcast_to(wk.reshape(r, 1, h, LANES), (r, gb, h, LANES)).reshape(r * gb * h, LANES)
        bb = jnp.where(first, bbr, bbi)
        mt = _dot_nt(bb.astype(BF16), wk_t.astype(BF16))
        toep_ref[gi * h:(gi + 1) * h, :] = jnp.where(lane_group == gi, mt, 0.0).astype(toep_ref.dtype)
        lam_r.append(pr[r * h:r * h + 1, :])
        lam_i.append(pi[r * h:r * h + 1, :])

    def pairs(rows):
        return jnp.concatenate([jnp.where(first, rows[2 * q], rows[2 * q + 1]) for q in range(gb // 2)], axis=1)

    lam_ref[0:1, :] = pairs(lam_r)
    lam_ref[1:2, :] = pairs(lam_i)


def s5_prep(lam_re, lam_im, log_step, b_re, b_im, c_re, c_im, r=S5_R, gb=S5_GB):
    g = lam_re.shape[0]
    h = S5_GROUP
    nlb = g // gb
    dbl = lambda x: jnp.concatenate([x, x], axis=-1)
    lr2, li2 = dbl(lam_re), dbl(lam_im)
    ls2 = jnp.broadcast_to(log_step[:, None], (g, LANES))
    bre2 = dbl(jnp.swapaxes(b_re, 1, 2))
    bim2 = dbl(jnp.swapaxes(b_im, 1, 2))
    cre2, cim2 = dbl(c_re), dbl(c_im)
    vec = pl.BlockSpec((gb, LANES), lambda i: (i, 0))
    mat = pl.BlockSpec((gb, h, LANES), lambda i: (i, 0, 0))
    tspec = pl.BlockSpec((None, 2, r, gb, h, LANES), lambda i: (i, 0, 0, 0, 0, 0))
    tshape = jax.ShapeDtypeStruct((nlb, 2, r, gb, h, LANES), BF16)
    kern = functools.partial(_s5_prep_kernel, gb=gb, r=r)
    tin, tout, toep, lam = pl.pallas_call(
        kern,
        grid=(nlb,),
        in_specs=[vec, vec, vec, mat, mat, mat, mat],
        out_specs=[tspec, tspec,
                   pl.BlockSpec((None, gb * h, r * LANES), lambda i: (i, 0, 0)),
                   pl.BlockSpec((None, 2, gb // 2 * LANES), lambda i: (i, 0, 0))],
        out_shape=[tshape, tshape,
                   jax.ShapeDtypeStruct((nlb, gb * h, r * LANES), BF16),
                   jax.ShapeDtypeStruct((nlb, 2, gb // 2 * LANES), F32)],
        compiler_params=_cparams(("parallel",)),
        name="s5_prep",
    )(lr2, li2, ls2, bre2, bim2, cre2, cim2)
    rows = r * gb * h
    return tin.reshape(nlb, 2, rows, LANES), tout.reshape(nlb, 2, rows, LANES), toep, lam


def _s5_pair_mask(r, gb):
    group = (np.arange(r * LANES) % LANES) // S5_GROUP
    lane = np.arange(gb // 2 * LANES)
    target = 2 * (lane // LANES) + (lane % LANES) // S5_STATE
    return jnp.asarray((group[:, None] == target[None, :]).astype(np.float32), dtype=BF16)


def _s5_kernel(a_ref, tin_ref, tout_ref, toep_ref, lam_ref, mask_ref, d_ref, y_ref,
               a32_scr, y32_scr, bdin_scr, bdout_scr, bdt_scr, s_scr, xp_scr, *, batch, jn, r, gb):
    bj = batch * jn
    npair = gb // 2
    half = npair * LANES
    a32_scr[...] = a_ref[...].astype(F32)
    a_tok = [a32_scr[pl.ds(i, bj, stride=r), :] for i in range(r)]

    for q in range(npair):
        mq = mask_ref[:, q * LANES:(q + 1) * LANES]
        for part in range(2):
            cols = slice((part * npair + q) * LANES, (part * npair + q + 1) * LANES)
            bdin_scr[:, cols] = tin_ref[part] * mq
            bdout_scr[:, cols] = tout_ref[part] * mq
    for i in range(r):
        if i > 0:
            bdt_scr[i * LANES:(i + 1) * LANES, :i * LANES] = jnp.zeros((LANES, i * LANES), bdt_scr.dtype)
        bdt_scr[i * LANES:(i + 1) * LANES, i * LANES:] = toep_ref[:, :(r - i) * LANES]

    a = jnp.concatenate([x.astype(BF16) for x in a_tok], axis=1)
    s_scr[...] = _dot(a, bdin_scr[...])
    lam_r = lam_ref[0:1, :]
    lam_i = lam_ref[1:2, :]

    def step(j, carry):
        out = []
        for b in range(batch):
            xr, xi = carry[2 * b], carry[2 * b + 1]
            row = pl.ds(b * jn + j, 1)
            xp_scr[row, :half] = xr
            xp_scr[row, half:] = xi
            s = s_scr[row, :]
            out.append(lam_r * xr - lam_i * xi + s[:, :half])
            out.append(lam_r * xi + lam_i * xr + s[:, half:])
        return tuple(out)

    zero = jnp.zeros((1, half), F32)
    lax.fori_loop(0, jn, step, (zero,) * (2 * batch))

    y_state = _dot_nt(xp_scr[...].astype(BF16), bdout_scr[...])
    d = d_ref[...]
    span = r // S5_TOEP_SPLIT
    for g in range(S5_TOEP_SPLIT):
        rows = (g + 1) * span * LANES
        cols = slice(g * span * LANES, rows)
        y = y_state[:, cols] + _dot(a[:, :rows], bdt_scr[:rows, cols])
        for i in range(g * span, (g + 1) * span):
            yi = y[:, (i - g * span) * LANES:(i - g * span + 1) * LANES] + a_tok[i] * d
            y32_scr[pl.ds(i, bj, stride=r), :] = 0.5 * yi * (1.0 + lax.erf(yi * (2.0 ** -0.5)))
    y_ref[...] = y32_scr[...].astype(y_ref.dtype)


def s5_mixer_gelu(a_norm, batch, seq, lam_re, lam_im, log_step, b_re, b_im, c_re, c_im, d_skip,
                  r=S5_R, gb=S5_GB):
    m, d = a_norm.shape
    assert gb * S5_GROUP == LANES
    nlb = d // LANES
    jn = seq // r
    bj = batch * jn
    half = gb // 2 * LANES
    tin, tout, toep, lam = s5_prep(lam_re, lam_im, log_step, b_re, b_im, c_re, c_im, r=r, gb=gb)
    lane_block = pl.BlockSpec((m, LANES), lambda lb: (0, lb))
    whole = lambda *shape: pl.BlockSpec((None,) + shape, lambda lb: (lb,) + (0,) * len(shape))
    kern = functools.partial(_s5_kernel, batch=batch, jn=jn, r=r, gb=gb)
    return pl.pallas_call(
        kern,
        grid=(nlb,),
        in_specs=[lane_block,
                  whole(2, r * LANES, LANES), whole(2, r * LANES, LANES), whole(LANES, r * LANES), whole(2, half),
                  pl.BlockSpec((r * LANES, half), lambda lb: (0, 0)),
                  pl.BlockSpec((1, LANES), lambda lb: (0, lb))],
        out_specs=lane_block,
        out_shape=jax.ShapeDtypeStruct((m, d), BF16),
        scratch_shapes=[pltpu.VMEM((m, LANES), F32), pltpu.VMEM((m, LANES), F32),
                        pltpu.VMEM((r * LANES, 2 * half), BF16), pltpu.VMEM((r * LANES, 2 * half), BF16),
                        pltpu.VMEM((r * LANES, r * LANES), BF16),
                        pltpu.VMEM((bj, 2 * half), F32), pltpu.VMEM((bj, 2 * half), F32)],
        compiler_params=_cparams(("parallel",)),
        name="s5_blocks",
    )(a_norm, tin, tout, toep, lam, _s5_pair_mask(r, gb), d_skip.reshape(1, d))


def kernel(x, norm_gains, w_in, w_gate_up, b_gate, gla_norm_gain, w_out, s5_lambda_re, s5_lambda_im,
           s5_log_step, s5_b_re, s5_b_im, s5_c_re, s5_c_im, s5_d, w_glu, w_ffn_in, w_ffn_out):
    batch, seq, d = x.shape
    m = batch * seq
    depth = norm_gains.shape[0]
    sb_w = SB_HEADS * SB_HEAD_DIM
    main_w = 3 * sb_w + 2 * GLA_HEADS * GLA_DK + 2 * GLA_HEADS * GLA_DV

    w_in_b = w_in[:, :, :main_w].astype(BF16)
    w_lr = jnp.pad(w_in[:, :, main_w:], ((0, 0), (0, 0), (0, LANES - GLA_GATE_RANK)))
    w_out_b, w_glu_b, w_ffn_out_b = w_out.astype(BF16), w_glu.astype(BF16), w_ffn_out.astype(BF16)

    h = x.astype(F32).reshape(m, d)
    a = rms_norm_bf16(h, norm_gains[0, 0])
    for layer in range(depth):
        gains = norm_gains[layer]
        i = layer // 2
        if layer % 2 == 0:
            proj = matmul(a, w_in_b, i, main_w, BF16, tm=1024, tn=1024)
            glr = matmul(a, w_lr, i, LANES, F32, tm=1024, tn=LANES)
            o_sb = sb_attention(proj, batch, seq)
            w_gate_pad = jnp.pad(w_gate_up[i], ((0, LANES - GLA_GATE_RANK), (0, 0)))
            o_gla = gla_attention(proj, glr, w_gate_pad, b_gate[i], gla_norm_gain[i], batch, seq, 3 * sb_w)
            h, a = matmul_residual((o_sb, o_gla), w_out_b, i, h, gains[1], gains[2])
        else:
            y = s5_mixer_gelu(a, batch, seq, s5_lambda_re[i], s5_lambda_im[i], s5_log_step[i],
                              s5_b_re[i], s5_b_im[i], s5_c_re[i], s5_c_im[i], s5_d[i])
            h, a = matmul_residual(y, w_glu_b, i, h, gains[1], gains[2], glu=True, tk=d // 2)
        f = ffn_in(a, w_ffn_in, layer)
        g_next = norm_gains[layer + 1, 0] if layer + 1 < depth else gains[3]
        h, a = matmul_residual(f, w_ffn_out_b, layer, h, gains[3], g_next, tk=w_ffn_out.shape[1] // 4)
    return h.reshape(batch, seq, d).astype(x.dtype)
```

```python
import functools
import math

import numpy as np
import jax
import jax.numpy as jnp
from jax import lax
from jax.experimental import pallas as pl
from jax.experimental.pallas import tpu as pltpu

F32 = jnp.float32
BF16 = jnp.bfloat16

NORM_EPS = 1e-6
LANES = 128
SB_HEADS = 8
SB_HEAD_DIM = 128
GLA_HEADS = 8
GLA_DK = 64
GLA_DV = 128
GLA_GATE_RANK = 16
GLA_GATE_TAU = 16.0
S5_GROUP = 16
S5_STATE = 64

SB_TQ = 512
SB_TK = 256
SB_TS = 256
GLA_CHUNK = 128
PROJ_TN = 1280
S5_R = 16
S5_GB = 8
S5_TOEP_SPLIT = 4

VMEM_LIMIT = 56 * 1024 * 1024

_HIGHEST = lax.Precision.HIGHEST
LOG2E = math.log2(math.e)


def _cparams(sem):
    return pltpu.CompilerParams(dimension_semantics=sem, vmem_limit_bytes=VMEM_LIMIT)


def _rms(x, g):
    return x * lax.rsqrt(jnp.mean(x * x, axis=-1, keepdims=True) + NORM_EPS) * g


def _dot(a, b):
    return jnp.dot(a, b, preferred_element_type=F32)


def _dot_nt(a, b):
    return lax.dot_general(a, b, (((1,), (1,)), ((), ())), preferred_element_type=F32)


def _dot_tn(a, b):
    return lax.dot_general(a, b, (((0,), (0,)), ((), ())), preferred_element_type=F32)


def _split_bf16(x):
    hi = x.astype(BF16)
    lo = (x - hi.astype(F32)).astype(BF16)
    return hi, lo


def _norm_kernel(h_ref, g_ref, o_ref):
    o_ref[...] = _rms(h_ref[...], g_ref[...]).astype(o_ref.dtype)


def rms_norm_bf16(h, g, tm=512):
    m, d = h.shape
    return pl.pallas_call(
        _norm_kernel,
        grid=(m // tm,),
        in_specs=[pl.BlockSpec((tm, d), lambda i: (i, 0)),
                  pl.BlockSpec((1, d), lambda i: (0, 0))],
        out_specs=pl.BlockSpec((tm, d), lambda i: (i, 0)),
        out_shape=jax.ShapeDtypeStruct((m, d), BF16),
        compiler_params=_cparams(("parallel",)),
        name="rms_norm",
    )(h, g.reshape(1, d))


def _mm_kernel(a_ref, w_ref, o_ref, wb_ref):
    @pl.when(pl.program_id(1) == 0)
    def _():
        wb_ref[...] = w_ref[...].astype(BF16)

    o_ref[...] = _dot(a_ref[...], wb_ref[...]).astype(o_ref.dtype)


def matmul(a, w, layer, ncols, out_dtype, tm, tn):
    m, k = a.shape
    return pl.pallas_call(
        _mm_kernel,
        grid=(ncols // tn, m // tm),
        in_specs=[pl.BlockSpec((tm, k), lambda j, i: (i, 0)),
                  pl.BlockSpec((None, k, tn), lambda j, i: (layer, 0, j))],
        out_specs=pl.BlockSpec((tm, tn), lambda j, i: (i, j)),
        out_shape=jax.ShapeDtypeStruct((m, ncols), out_dtype),
        scratch_shapes=[pltpu.VMEM((k, tn), BF16)],
        compiler_params=_cparams(("parallel", "arbitrary")),
        name="matmul",
    )(a, w)


def _ffn_in_kernel(a_ref, wg_ref, wu_ref, o_ref, wgb_ref, wub_ref):
    @pl.when(pl.program_id(1) == 0)
    def _():
        wgb_ref[...] = wg_ref[...].astype(BF16)
        wub_ref[...] = wu_ref[...].astype(BF16)

    a = a_ref[...]
    g = _dot(a, wgb_ref[...])
    u = _dot(a, wub_ref[...])
    o_ref[...] = (g * jax.nn.sigmoid(g) * u).astype(o_ref.dtype)


def ffn_in(a, w, layer, tm=2048, tn=512):
    m, k = a.shape
    nf = w.shape[2] // 2
    nj = nf // tn
    return pl.pallas_call(
        _ffn_in_kernel,
        grid=(nj, m // tm),
        in_specs=[pl.BlockSpec((tm, k), lambda j, i: (i, 0)),
                  pl.BlockSpec((None, k, tn), lambda j, i: (layer, 0, j)),
                  pl.BlockSpec((None, k, tn), lambda j, i: (layer, 0, j + nj))],
        out_specs=pl.BlockSpec((tm, tn), lambda j, i: (i, j)),
        out_shape=jax.ShapeDtypeStruct((m, nf), BF16),
        scratch_shapes=[pltpu.VMEM((k, tn), BF16), pltpu.VMEM((k, tn), BF16)],
        compiler_params=_cparams(("parallel", "arbitrary")),
        name="ffn_in",
    )(a, w, w)


def _mm_res_kernel(*refs, nt, nk, glu, d, n_a):
    a_refs = refs[:n_a]
    w_ref, h_ref, gpost_ref, gnext_ref, hout_ref, anext_ref = refs[n_a:n_a + 6]
    accs = refs[n_a + 6:]
    i = pl.program_id(0)
    k = pl.program_id(1)

    def product():
        if n_a == 1:
            return _dot(a_refs[0][...], w_ref[...])
        out, lo = None, 0
        for ar in a_refs:
            part = _dot(ar[...], w_ref[lo:lo + ar.shape[1], :])
            out = part if out is None else out + part
            lo += ar.shape[1]
        return out

    def finish(y):
        if glu:
            y = y[:, :d] * jax.nn.sigmoid(y[:, d:])
        hn = h_ref[...] + _rms(y, gpost_ref[...])
        hout_ref[...] = hn
        anext_ref[...] = _rms(hn, gnext_ref[...]).astype(anext_ref.dtype)

    @pl.when((i == 0) & (k == 0))
    def _():
        accs[1][...] = jnp.zeros_like(accs[1])

    for parity in range(2):
        mine, other = accs[parity], accs[1 - parity]

        @pl.when((k == 0) & (i % 2 == parity))
        def _(mine=mine, other=other):
            finish(other[...])
            mine[...] = product()

        @pl.when((k > 0) & (i < nt) & (i % 2 == parity))
        def _(mine=mine):
            mine[...] += product()


def matmul_residual(a, w, layer, h, g_post, g_next, *, glu=False, tm=512, tk=512):
    a_list = list(a) if isinstance(a, (tuple, list)) else [a]
    n_a = len(a_list)
    m = a_list[0].shape[0]
    kdim, n = w.shape[1:]
    d = h.shape[1]
    nt = m // tm
    if n_a > 1:
        tk = kdim
        assert sum(x.shape[1] for x in a_list) == kdim
    nk = kdim // tk
    row = lambda i: jnp.minimum(i, nt - 1)
    kstep = lambda i, k: jnp.where(i == nt, nk - 1, k)
    prev = lambda i: jnp.maximum(i - 1, 0)
    if n_a == 1:
        a_specs = [pl.BlockSpec((tm, tk), lambda i, k: (row(i), kstep(i, k)))]
    else:
        a_specs = [pl.BlockSpec((tm, x.shape[1]), lambda i, k: (row(i), 0)) for x in a_list]
    kern = functools.partial(_mm_res_kernel, nt=nt, nk=nk, glu=glu, d=d, n_a=n_a)
    return pl.pallas_call(
        kern,
        grid=(nt + 1, nk),
        in_specs=a_specs + [
                  pl.BlockSpec((None, tk, n), lambda i, k: (layer, kstep(i, k), 0)),
                  pl.BlockSpec((tm, d), lambda i, k: (prev(i), 0)),
                  pl.BlockSpec((1, d), lambda i, k: (0, 0)),
                  pl.BlockSpec((1, d), lambda i, k: (0, 0))],
        out_specs=[pl.BlockSpec((tm, d), lambda i, k: (prev(i), 0)),
                   pl.BlockSpec((tm, d), lambda i, k: (prev(i), 0))],
        out_shape=[jax.ShapeDtypeStruct((m, d), F32),
                   jax.ShapeDtypeStruct((m, d), BF16)],
        scratch_shapes=[pltpu.VMEM((tm, n), F32), pltpu.VMEM((tm, n), F32)],
        compiler_params=_cparams(("arbitrary", "arbitrary")),
        name="matmul_residual",
    )(*a_list, w, h, g_post.reshape(1, d), g_next.reshape(1, d))


def _sb_kernel(q_ref, k_ref, v_ref, u_ref, o_ref, acc_ref, c_ref, *, tq, tk, ts, scale):
    i = pl.program_id(2)
    q = (q_ref[...].astype(F32) * scale).astype(BF16)
    u = u_ref[...]
    acc_ref[...] = jnp.zeros_like(acc_ref)
    c_ref[...] = jnp.zeros_like(c_ref)

    def run(items):
        def rows(ref, it):
            return ref[pl.ds(pl.multiple_of(it[0] * tk, tk), tk), :]

        def logits(it):
            return _dot_nt(rows(k_ref, it), q[it[2], :])

        def keep_mask(it, shape):
            spos = it[1] + lax.broadcasted_iota(jnp.int32, shape, 0)
            tpos = it[2].start + lax.broadcasted_iota(jnp.int32, shape, 1)
            return spos < tpos

        def suffix(it, z):
            ls = jnp.minimum(z, 0.0) - jnp.log(1.0 + jnp.exp2(jnp.abs(z) * (-LOG2E)))
            lk = ls - z
            if it[1] is not None:
                lk = jnp.where(keep_mask(it, z.shape), lk, 0.0)
            later = _dot(u, lk.astype(BF16))
            return ls, later, later[0:1, :] + lk[0:1, :]

        def weighted(it, ls, later):
            w = jnp.exp(ls + later)
            if it[1] is not None:
                w = jnp.where(keep_mask(it, w.shape), w, 0.0)
            return _dot_tn(rows(v_ref, it), w.astype(BF16))

        zs = [logits(it) for it in items]
        sfx = [None] * len(items)
        pvs = [None] * len(items)
        sfx[0] = suffix(items[0], zs[0])
        for n in range(1, len(items)):
            sfx[n] = suffix(items[n], zs[n])
            pvs[n - 1] = weighted(items[n - 1], sfx[n - 1][0], sfx[n - 1][1])
        pvs[-1] = weighted(items[-1], sfx[-1][0], sfx[-1][1])
        for it, (_, _, total), pv in zip(items, sfx, pvs):
            acc_ref[:, it[2]] += jnp.exp(c_ref[:, it[2]]) * pv
            c_ref[:, it[2]] += total

    nstraddle = tq // tk
    items = []
    for s in range(nstraddle - 1, -1, -1):
        for j in range(tq // ts):
            k_lo, q_lo = s * tk, j * ts
            if k_lo >= q_lo + ts - 1:
                continue
            items.append((i * nstraddle + s, k_lo if k_lo + tk > q_lo else None,
                          slice(q_lo, q_lo + ts)))
    run(items)

    def full_blocks(first, count):
        run([(first - n, None, slice(0, tq)) for n in range(count)])

    nfull = i * nstraddle
    group = 2 * nstraddle

    @pl.when(nfull % group != 0)
    def _():
        full_blocks(nfull - 1, nstraddle)

    def body(n, carry):
        full_blocks(nfull - nfull % group - 1 - group * n, group)
        return carry

    lax.fori_loop(0, nfull // group, body, 0)
    o_ref[...] = acc_ref[...].T.astype(o_ref.dtype)


def _sb_umat(tk):
    s = np.arange(tk)[:, None]
    j = np.arange(tk)[None, :]
    return jnp.asarray((j > s).astype(np.float32), dtype=BF16)


def sb_attention(proj, batch, seq, tq=SB_TQ, tk=SB_TK):
    d, nh = SB_HEAD_DIM, SB_HEADS
    assert (tq // tk) % 2 == 0
    nq = seq // tq
    kern = functools.partial(_sb_kernel, tq=tq, tk=tk, ts=SB_TS, scale=d ** -0.5)
    return pl.pallas_call(
        kern,
        grid=(batch, nh, nq),
        in_specs=[pl.BlockSpec((tq, d), lambda b, h, i: (b * nq + i, h)),
                  pl.BlockSpec((seq, d), lambda b, h, i: (b, nh + h)),
                  pl.BlockSpec((seq, d), lambda b, h, i: (b, 2 * nh + h)),
                  pl.BlockSpec((tk, tk), lambda b, h, i: (0, 0))],
        out_specs=pl.BlockSpec((tq, d), lambda b, h, i: (b * nq + i, h)),
        out_shape=jax.ShapeDtypeStruct((batch * seq, nh * d), BF16),
        scratch_shapes=[pltpu.VMEM((d, tq), F32), pltpu.VMEM((1, tq), F32)],
        compiler_params=_cparams(("parallel", "parallel", "parallel")),
        name="sb_attention",
    )(proj, proj, proj, _sb_umat(tk))


def _gla_levels(c):
    return int(math.log2(c))


def _gla_masks(c):
    t = np.arange(c)[:, None]
    s = np.arange(c)[None, :]
    masks = []
    for lev in range(_gla_levels(c)):
        half = c >> (lev + 1)
        blk = 2 * half
        masks.append((t // blk == s // blk) & ((t % blk) >= half) & ((s % blk) < half))
    masks.append(t == s)
    m = np.concatenate(masks, axis=0).astype(np.float32)
    return jnp.asarray(np.concatenate([m, m], axis=1))


def _gla_boundary_rows(cum, half):
    c, width = cum.shape
    blk = 2 * half
    sub = 8
    if half >= sub:
        return jnp.concatenate(
            [jnp.broadcast_to(cum[b * blk + half - 1:b * blk + half, :], (blk, width)) for b in range(c // blk)],
            axis=0)
    x = cum.reshape(c // sub, sub, width)
    row = lax.broadcasted_iota(jnp.int32, (c // sub, sub, width), 1)
    out = None
    for b in range(sub // blk):
        piece = jnp.broadcast_to(x[:, b * blk + half - 1:b * blk + half, :], x.shape)
        out = piece if out is None else jnp.where(row >= b * blk, piece, out)
    return out.reshape(c, width)


def _gla_kernel(q_ref, k_ref, v_ref, r_ref, glr_ref, wg_ref, bg_ref, gain_ref, t_ref, m_ref, bm_ref,
                o_ref, st_ref, *, c, nlev):
    @pl.when(pl.program_id(1) == 0)
    def _():
        st_ref[...] = jnp.zeros_like(st_ref)

    dk, dv = GLA_DK, GLA_DV
    npair = GLA_HEADS // 2
    logits = jnp.dot(glr_ref[...].astype(F32), wg_ref[...], precision=_HIGHEST,
                     preferred_element_type=F32) + bg_ref[...]
    la = jax.nn.log_sigmoid(logits) / GLA_GATE_TAU
    hi, lo = _split_bf16(la)
    cum = _dot(t_ref[...], hi) + _dot(t_ref[...], lo)
    total = cum[c - 1:c, :]
    f_in = jnp.exp(cum)
    f_out = jnp.exp(total - cum)
    f_lev = [jnp.exp(-jnp.abs(cum - _gla_boundary_rows(cum, c >> (lev + 1)))) for lev in range(nlev)]

    q = q_ref[...].astype(F32) * (dk ** -0.5)
    k = k_ref[...].astype(F32)
    qs = [(q * f).astype(BF16) for f in f_lev] + [q.astype(BF16)]
    ks = [(k * f).astype(BF16) for f in f_lev] + [k.astype(BF16)]
    q_in = (q * f_in).astype(BF16)
    k_out = (k * f_out).astype(BF16)
    ones = jnp.ones((c, 2 * dv), BF16)
    head_a = lax.broadcasted_iota(jnp.int32, (c, 2 * dk), 1) < dk
    zero_k = jnp.zeros((c, 2 * dk), BF16)
    zero_v = jnp.zeros((c, dv), BF16)

    for p in range(npair):
        lanes = slice(p * 2 * dk, (p + 1) * 2 * dk)
        wide = slice(p * 2 * dv, (p + 1) * 2 * dv)
        scores = jnp.zeros((c, 2 * c), F32)
        for lev in range(nlev + 1):
            kp = ks[lev][:, lanes]
            kstack = jnp.concatenate([jnp.where(head_a, kp, zero_k), jnp.where(head_a, zero_k, kp)], axis=0)
            scores = scores + m_ref[lev * c:(lev + 1) * c, :] * _dot_nt(qs[lev][:, lanes], kstack)
        v = v_ref[:, wide]
        v_bd = jnp.concatenate([jnp.concatenate([v[:, :dv], zero_v], axis=1),
                                jnp.concatenate([zero_v, v[:, dv:]], axis=1)], axis=0)
        st = st_ref[p]
        o = _dot(scores.astype(BF16), v_bd) + _dot(q_in[:, lanes], st.astype(BF16))
        tot_col = _dot_tn(hi[:, lanes], ones) + _dot_tn(lo[:, lanes], ones)
        st_ref[p] = jnp.exp(tot_col) * st + bm_ref[...] * _dot_tn(k_out[:, lanes], v)
        for hd in range(2):
            oh = o[:, hd * dv:(hd + 1) * dv]
            oh = oh * lax.rsqrt(jnp.mean(oh * oh, axis=-1, keepdims=True) + NORM_EPS)
            cols = slice(p * 2 * dv + hd * dv, p * 2 * dv + (hd + 1) * dv)
            rr = r_ref[:, cols].astype(F32)
            o_ref[:, cols] = (oh * gain_ref[:, cols] * (rr * jax.nn.sigmoid(rr))).astype(o_ref.dtype)


def gla_attention(proj, w_gate_pad, b_gate, gain, batch, seq, col0, c=GLA_CHUNK):
    dk, dv, nh = GLA_DK, GLA_DV, GLA_HEADS
    nc = seq // c
    nlev = _gla_levels(c)
    kw, vw = nh * dk, nh * dv
    qb = col0 // kw
    vb = (col0 + 2 * kw) // vw
    lrb = (col0 + 2 * kw + 2 * vw) // LANES
    tincl = jnp.asarray(np.tril(np.ones((c, c), np.float32)), dtype=BF16)
    pair_blocks = jnp.asarray(np.kron(np.eye(2, dtype=np.float32), np.ones((dk, dv), np.float32)))
    kern = functools.partial(_gla_kernel, c=c, nlev=nlev)
    row = lambda b, n: b * nc + n
    const = lambda shape: pl.BlockSpec(shape, lambda b, n: (0, 0))
    return pl.pallas_call(
        kern,
        grid=(batch, nc),
        in_specs=[pl.BlockSpec((c, kw), lambda b, n: (row(b, n), qb)),
                  pl.BlockSpec((c, kw), lambda b, n: (row(b, n), qb + 1)),
                  pl.BlockSpec((c, vw), lambda b, n: (row(b, n), vb)),
                  pl.BlockSpec((c, vw), lambda b, n: (row(b, n), vb + 1)),
                  pl.BlockSpec((c, LANES), lambda b, n: (row(b, n), lrb)),
                  const((LANES, kw)), const((1, kw)), const((1, vw)),
                  const((c, c)), const(((nlev + 1) * c, 2 * c)), const((2 * dk, 2 * dv))],
        out_specs=pl.BlockSpec((c, vw), lambda b, n: (row(b, n), 0)),
        out_shape=jax.ShapeDtypeStruct((batch * seq, vw), BF16),
        scratch_shapes=[pltpu.VMEM((nh // 2, 2 * dk, 2 * dv), F32)],
        compiler_params=_cparams(("parallel", "arbitrary")),
        name="gla",
    )(proj, proj, proj, proj, proj, w_gate_pad, b_gate.reshape(1, -1), gain.reshape(1, -1),
      tincl, _gla_masks(c), pair_blocks)


def _s5_prep_kernel(lr_ref, li_ref, ls_ref, bre_ref, bim_ref, cre_ref, cim_ref,
                    tin_ref, tout_ref, toep_ref, lam_ref, *, gb, r):
    h = S5_GROUP
    r16 = r * h
    first = lax.broadcasted_iota(jnp.int32, (1, LANES), 1) < S5_STATE
    npow = -(-(r + 1) // 8) * 8
    kidx = lax.broadcasted_iota(jnp.int32, (npow, LANES), 0).astype(F32)
    lane_group = (lax.broadcasted_iota(jnp.int32, (h, r * LANES), 1) % LANES) // h

    def tile_rows(x):
        return jnp.broadcast_to(x[None], (r, h, LANES)).reshape(r16, LANES)

    def by_token(x):
        return x.reshape(r, h, LANES).astype(BF16)

    def rows_of(table, powers):
        return jnp.concatenate([jnp.broadcast_to(table[k:k + 1, :], (h, LANES)) for k in powers], axis=0)

    lam_r, lam_i = [], []
    for gi in range(gb):
        lr = lr_ref[gi:gi + 1, :]
        li = li_ref[gi:gi + 1, :]
        dt = jnp.exp(ls_ref[gi:gi + 1, :])
        mag = jnp.exp(kidx * (dt * lr))
        ang = kidx * (dt * li)
        pw_r, pw_i = mag * jnp.cos(ang), mag * jnp.sin(ang)
        lbr, lbi = pw_r[1:2, :], pw_i[1:2, :]
        den = lr * lr + li * li
        nr, ni = lbr - 1.0, lbi
        cr = (nr * lr + ni * li) / den
        ci = (ni * lr - nr * li) / den
        bre, bim = bre_ref[gi], bim_ref[gi]
        bbr = cr * bre - ci * bim
        bbi = cr * bim + ci * bre
        pr, pi = rows_of(pw_r, range(r + 1)), rows_of(pw_i, range(r + 1))
        prr, pir = rows_of(pw_r, range(r - 1, -1, -1)), rows_of(pw_i, range(r - 1, -1, -1))
        bbr_t, bbi_t = tile_rows(bbr), tile_rows(bbi)
        tin_ref[0, :, gi] = by_token(prr * bbr_t - pir * bbi_t)
        tin_ref[1, :, gi] = by_token(prr * bbi_t + pir * bbr_t)
        cr_t, ci_t = tile_rows(cre_ref[gi]), tile_rows(cim_ref[gi])
        pr1, pi1 = pr[h:], pi[h:]
        tout_ref[0, :, gi] = by_token(cr_t * pr1 - ci_t * pi1)
        tout_ref[1, :, gi] = by_token(-(cr_t * pi1 + ci_t * pr1))
        pr0, pi0 = pr[:r16], pi[:r16]
        wk = jnp.where(first, cr_t * pr0 - ci_t * pi0, -(cr_t * pi0 + ci_t * pr0))
        wk_t = jnp.broadcast_to(wk.reshape(r, 1, h, LANES), (r, gb, h, LANES)).reshape(r * gb * h, LANES)
        bb = jnp.where(first, bbr, bbi)
        mt = _dot_nt(bb.astype(BF16), wk_t.astype(BF16))
        toep_ref[gi * h:(gi + 1) * h, :] = jnp.where(lane_group == gi, mt, 0.0).astype(toep_ref.dtype)
        lam_r.append(pr[r * h:r * h + 1, :])
        lam_i.append(pi[r * h:r * h + 1, :])

    def pairs(rows):
        return jnp.concatenate([jnp.where(first, rows[2 * q], rows[2 * q + 1]) for q in range(gb // 2)], axis=1)

    lam_ref[0:1, :] = pairs(lam_r)
    lam_ref[1:2, :] = pairs(lam_i)


def s5_prep(lam_re, lam_im, log_step, b_re, b_im, c_re, c_im, r=S5_R, gb=S5_GB):
    g = lam_re.shape[0]
    h = S5_GROUP
    nlb = g // gb
    dbl = lambda x: jnp.concatenate([x, x], axis=-1)
    lr2, li2 = dbl(lam_re), dbl(lam_im)
    ls2 = jnp.broadcast_to(log_step[:, None], (g, LANES))
    bre2 = dbl(jnp.swapaxes(b_re, 1, 2))
    bim2 = dbl(jnp.swapaxes(b_im, 1, 2))
    cre2, cim2 = dbl(c_re), dbl(c_im)
    vec = pl.BlockSpec((gb, LANES), lambda i: (i, 0))
    mat = pl.BlockSpec((gb, h, LANES), lambda i: (i, 0, 0))
    tspec = pl.BlockSpec((None, 2, r, gb, h, LANES), lambda i: (i, 0, 0, 0, 0, 0))
    tshape = jax.ShapeDtypeStruct((nlb, 2, r, gb, h, LANES), BF16)
    kern = functools.partial(_s5_prep_kernel, gb=gb, r=r)
    tin, tout, toep, lam = pl.pallas_call(
        kern,
        grid=(nlb,),
        in_specs=[vec, vec, vec, mat, mat, mat, mat],
        out_specs=[tspec, tspec,
                   pl.BlockSpec((None, gb * h, r * LANES), lambda i: (i, 0, 0)),
                   pl.BlockSpec((None, 2, gb // 2 * LANES), lambda i: (i, 0, 0))],
        out_shape=[tshape, tshape,
                   jax.ShapeDtypeStruct((nlb, gb * h, r * LANES), BF16),
                   jax.ShapeDtypeStruct((nlb, 2, gb // 2 * LANES), F32)],
        compiler_params=_cparams(("parallel",)),
        name="s5_prep",
    )(lr2, li2, ls2, bre2, bim2, cre2, cim2)
    rows = r * gb * h
    return tin.reshape(nlb, 2, rows, LANES), tout.reshape(nlb, 2, rows, LANES), toep, lam


def _s5_pair_mask(r, gb):
    group = (np.arange(r * LANES) % LANES) // S5_GROUP
    lane = np.arange(gb // 2 * LANES)
    target = 2 * (lane // LANES) + (lane % LANES) // S5_STATE
    return jnp.asarray((group[:, None] == target[None, :]).astype(np.float32), dtype=BF16)


def _s5_kernel(a_ref, tin_ref, tout_ref, toep_ref, lam_ref, mask_ref, d_ref, y_ref,
               a32_scr, y32_scr, bdin_scr, bdout_scr, bdt_scr, s_scr, xp_scr, *, batch, jn, r, gb):
    bj = batch * jn
    npair = gb // 2
    half = npair * LANES
    a32_scr[...] = a_ref[...].astype(F32)
    a_tok = [a32_scr[pl.ds(i, bj, stride=r), :] for i in range(r)]

    for q in range(npair):
        mq = mask_ref[:, q * LANES:(q + 1) * LANES]
        for part in range(2):
            cols = slice((part * npair + q) * LANES, (part * npair + q + 1) * LANES)
            bdin_scr[:, cols] = tin_ref[part] * mq
            bdout_scr[:, cols] = tout_ref[part] * mq
    for i in range(r):
        if i > 0:
            bdt_scr[i * LANES:(i + 1) * LANES, :i * LANES] = jnp.zeros((LANES, i * LANES), bdt_scr.dtype)
        bdt_scr[i * LANES:(i + 1) * LANES, i * LANES:] = toep_ref[:, :(r - i) * LANES]

    a = jnp.concatenate([x.astype(BF16) for x in a_tok], axis=1)
    s_scr[...] = _dot(a, bdin_scr[...])
    lam_r = lam_ref[0:1, :]
    lam_i = lam_ref[1:2, :]

    def step(j, carry):
        out = []
        for b in range(batch):
            xr, xi = carry[2 * b], carry[2 * b + 1]
            row = pl.ds(b * jn + j, 1)
            xp_scr[row, :half] = xr
            xp_scr[row, half:] = xi
            s = s_scr[row, :]
            out.append(lam_r * xr - lam_i * xi + s[:, :half])
            out.append(lam_r * xi + lam_i * xr + s[:, half:])
        return tuple(out)

    zero = jnp.zeros((1, half), F32)
    lax.fori_loop(0, jn, step, (zero,) * (2 * batch))

    y_state = _dot_nt(xp_scr[...].astype(BF16), bdout_scr[...])
    d = d_ref[...]
    span = r // S5_TOEP_SPLIT
    for g in range(S5_TOEP_SPLIT):
        rows = (g + 1) * span * LANES
        cols = slice(g * span * LANES, rows)
        y = y_state[:, cols] + _dot(a[:, :rows], bdt_scr[:rows, cols])
        for i in range(g * span, (g + 1) * span):
            yi = y[:, (i - g * span) * LANES:(i - g * span + 1) * LANES] + a_tok[i] * d
            y32_scr[pl.ds(i, bj, stride=r), :] = 0.5 * yi * (1.0 + lax.erf(yi * (2.0 ** -0.5)))
    y_ref[...] = y32_scr[...].astype(y_ref.dtype)


def s5_mixer_gelu(a_norm, batch, seq, lam_re, lam_im, log_step, b_re, b_im, c_re, c_im, d_skip,
                  r=S5_R, gb=S5_GB):
    m, d = a_norm.shape
    assert gb * S5_GROUP == LANES
    nlb = d // LANES
    jn = seq // r
    bj = batch * jn
    half = gb // 2 * LANES
    tin, tout, toep, lam = s5_prep(lam_re, lam_im, log_step, b_re, b_im, c_re, c_im, r=r, gb=gb)
    lane_block = pl.BlockSpec((m, LANES), lambda lb: (0, lb))
    whole = lambda *shape: pl.BlockSpec((None,) + shape, lambda lb: (lb,) + (0,) * len(shape))
    kern = functools.partial(_s5_kernel, batch=batch, jn=jn, r=r, gb=gb)
    return pl.pallas_call(
        kern,
        grid=(nlb,),
        in_specs=[lane_block,
                  whole(2, r * LANES, LANES), whole(2, r * LANES, LANES), whole(LANES, r * LANES), whole(2, half),
                  pl.BlockSpec((r * LANES, half), lambda lb: (0, 0)),
                  pl.BlockSpec((1, LANES), lambda lb: (0, lb))],
        out_specs=lane_block,
        out_shape=jax.ShapeDtypeStruct((m, d), BF16),
        scratch_shapes=[pltpu.VMEM((m, LANES), F32), pltpu.VMEM((m, LANES), F32),
                        pltpu.VMEM((r * LANES, 2 * half), BF16), pltpu.VMEM((r * LANES, 2 * half), BF16),
                        pltpu.VMEM((r * LANES, r * LANES), BF16),
                        pltpu.VMEM((bj, 2 * half), F32), pltpu.VMEM((bj, 2 * half), F32)],
        compiler_params=_cparams(("parallel",)),
        name="s5_blocks",
    )(a_norm, tin, tout, toep, lam, _s5_pair_mask(r, gb), d_skip.reshape(1, d))


def kernel(x, norm_gains, w_in, w_gate_up, b_gate, gla_norm_gain, w_out, s5_lambda_re, s5_lambda_im,
           s5_log_step, s5_b_re, s5_b_im, s5_c_re, s5_c_im, s5_d, w_glu, w_ffn_in, w_ffn_out):
    batch, seq, d = x.shape
    m = batch * seq
    depth = norm_gains.shape[0]
    sb_w = SB_HEADS * SB_HEAD_DIM

    proj_w = -(-w_in.shape[2] // PROJ_TN) * PROJ_TN
    w_in_b = jnp.pad(w_in, ((0, 0), (0, 0), (0, proj_w - w_in.shape[2]))).astype(BF16)
    w_out_b, w_glu_b, w_ffn_out_b = w_out.astype(BF16), w_glu.astype(BF16), w_ffn_out.astype(BF16)

    h = x.astype(F32).reshape(m, d)
    a = rms_norm_bf16(h, norm_gains[0, 0])
    for layer in range(depth):
        gains = norm_gains[layer]
        i = layer // 2
        if layer % 2 == 0:
            proj = matmul(a, w_in_b, i, proj_w, BF16, tm=1024, tn=PROJ_TN)
            o_sb = sb_attention(proj, batch, seq)
            w_gate_pad = jnp.pad(w_gate_up[i], ((0, LANES - GLA_GATE_RANK), (0, 0)))
            o_gla = gla_attention(proj, w_gate_pad, b_gate[i], gla_norm_gain[i], batch, seq, 3 * sb_w)
            h, a = matmul_residual((o_sb, o_gla), w_out_b, i, h, gains[1], gains[2])
        else:
            y = s5_mixer_gelu(a, batch, seq, s5_lambda_re[i], s5_lambda_im[i], s5_log_step[i],
                              s5_b_re[i], s5_b_im[i], s5_c_re[i], s5_c_im[i], s5_d[i])
            h, a = matmul_residual(y, w_glu_b, i, h, gains[1], gains[2], glu=True, tk=d // 4)
        f = ffn_in(a, w_ffn_in, layer)
        g_next = norm_gains[layer + 1, 0] if layer + 1 < depth else gains[3]
        h, a = matmul_residual(f, w_ffn_out_b, layer, h, gains[3], g_next, tk=w_ffn_out.shape[1] // 4)
    return h.reshape(batch, seq, d).astype(x.dtype)
```

```python
import functools
import math

import numpy as np
import jax
import jax.numpy as jnp
from jax import lax
from jax.experimental import pallas as pl
from jax.experimental.pallas import tpu as pltpu

F32 = jnp.float32
BF16 = jnp.bfloat16

NORM_EPS = 1e-6
LANES = 128
SB_HEADS = 8
SB_HEAD_DIM = 128
GLA_HEADS = 8
GLA_DK = 64
GLA_DV = 128
GLA_GATE_RANK = 16
GLA_GATE_TAU = 16.0
S5_GROUP = 16
S5_STATE = 64

SB_TQ = 512
SB_TK = 256
SB_TS = 256
GLA_CHUNK = 128
PROJ_TN = 1280
S5_R = 16
S5_GB = 8
S5_TOEP_SPLIT = 4

VMEM_LIMIT = 56 * 1024 * 1024

_HIGHEST = lax.Precision.HIGHEST
LOG2E = math.log2(math.e)


def _cparams(sem):
    return pltpu.CompilerParams(dimension_semantics=sem, vmem_limit_bytes=VMEM_LIMIT)


def _rms(x, g):
    return x * lax.rsqrt(jnp.mean(x * x, axis=-1, keepdims=True) + NORM_EPS) * g


def _dot(a, b):
    return jnp.dot(a, b, preferred_element_type=F32)


def _dot_nt(a, b):
    return lax.dot_general(a, b, (((1,), (1,)), ((), ())), preferred_element_type=F32)


def _dot_tn(a, b):
    return lax.dot_general(a, b, (((0,), (0,)), ((), ())), preferred_element_type=F32)


def _split_bf16(x):
    hi = x.astype(BF16)
    lo = (x - hi.astype(F32)).astype(BF16)
    return hi, lo


def _norm_kernel(h_ref, g_ref, o_ref):
    o_ref[...] = _rms(h_ref[...], g_ref[...]).astype(o_ref.dtype)


def rms_norm_bf16(h, g, tm=512):
    m, d = h.shape
    return pl.pallas_call(
        _norm_kernel,
        grid=(m // tm,),
        in_specs=[pl.BlockSpec((tm, d), lambda i: (i, 0)),
                  pl.BlockSpec((1, d), lambda i: (0, 0))],
        out_specs=pl.BlockSpec((tm, d), lambda i: (i, 0)),
        out_shape=jax.ShapeDtypeStruct((m, d), BF16),
        compiler_params=_cparams(("parallel",)),
        name="rms_norm",
    )(h, g.reshape(1, d))


def _mm_kernel(a_ref, w_ref, o_ref, wb_ref):
    @pl.when(pl.program_id(1) == 0)
    def _():
        wb_ref[...] = w_ref[...].astype(BF16)

    o_ref[...] = _dot(a_ref[...], wb_ref[...]).astype(o_ref.dtype)


def matmul(a, w, layer, ncols, out_dtype, tm, tn):
    m, k = a.shape
    return pl.pallas_call(
        _mm_kernel,
        grid=(ncols // tn, m // tm),
        in_specs=[pl.BlockSpec((tm, k), lambda j, i: (i, 0)),
                  pl.BlockSpec((None, k, tn), lambda j, i: (layer, 0, j))],
        out_specs=pl.BlockSpec((tm, tn), lambda j, i: (i, j)),
        out_shape=jax.ShapeDtypeStruct((m, ncols), out_dtype),
        scratch_shapes=[pltpu.VMEM((k, tn), BF16)],
        compiler_params=_cparams(("parallel", "arbitrary")),
        name="matmul",
    )(a, w)


def _ffn_in_kernel(a_ref, wg_ref, wu_ref, o_ref, wgb_ref, wub_ref):
    @pl.when(pl.program_id(1) == 0)
    def _():
        wgb_ref[...] = wg_ref[...].astype(BF16)
        wub_ref[...] = wu_ref[...].astype(BF16)

    a = a_ref[...]
    g = _dot(a, wgb_ref[...])
    u = _dot(a, wub_ref[...])
    o_ref[...] = (g * jax.nn.sigmoid(g) * u).astype(o_ref.dtype)


def ffn_in(a, w, layer, tm=2048, tn=512):
    m, k = a.shape
    nf = w.shape[2] // 2
    nj = nf // tn
    return pl.pallas_call(
        _ffn_in_kernel,
        grid=(nj, m // tm),
        in_specs=[pl.BlockSpec((tm, k), lambda j, i: (i, 0)),
                  pl.BlockSpec((None, k, tn), lambda j, i: (layer, 0, j)),
                  pl.BlockSpec((None, k, tn), lambda j, i: (layer, 0, j + nj))],
        out_specs=pl.BlockSpec((tm, tn), lambda j, i: (i, j)),
        out_shape=jax.ShapeDtypeStruct((m, nf), BF16),
        scratch_shapes=[pltpu.VMEM((k, tn), BF16), pltpu.VMEM((k, tn), BF16)],
        compiler_params=_cparams(("parallel", "arbitrary")),
        name="ffn_in",
    )(a, w, w)


def _mm_res_kernel(*refs, nk, glu, d, n_a, tm, manual_h):
    a_refs = refs[:n_a]
    w_ref, h_ref, gpost_ref, gnext_ref, hout_ref, anext_ref, acc_ref = refs[n_a:n_a + 7]
    i = pl.program_id(0)
    k = pl.program_id(1)

    def product():
        if n_a == 1:
            return _dot(a_refs[0][...], w_ref[...])
        out, lo = None, 0
        for ar in a_refs:
            part = _dot(ar[...], w_ref[lo:lo + ar.shape[1], :])
            out = part if out is None else out + part
            lo += ar.shape[1]
        return out

    def finish(h_tile):
        y = acc_ref[...]
        if glu:
            y = y[:, :d] * jax.nn.sigmoid(y[:, d:])
        hn = h_tile + _rms(y, gpost_ref[...])
        anext_ref[...] = _rms(hn, gnext_ref[...]).astype(anext_ref.dtype)
        return hn

    @pl.when(k == 0)
    def _():
        acc_ref[...] = product()

    @pl.when(k > 0)
    def _():
        acc_ref[...] += product()

    if not manual_h:
        @pl.when(k == nk - 1)
        def _():
            hout_ref[...] = finish(h_ref[...])
        return

    hbuf, sems = refs[n_a + 7:]
    nt = pl.num_programs(0)

    def tile_in(t):
        return pltpu.make_async_copy(h_ref.at[pl.ds(pl.multiple_of(t * tm, tm), tm), :], hbuf, sems.at[0])

    def tile_out(t):
        return pltpu.make_async_copy(hbuf, hout_ref.at[pl.ds(pl.multiple_of(t * tm, tm), tm), :], sems.at[1])

    @pl.when(k == 1)
    def _():
        @pl.when(i > 0)
        def _():
            tile_out(i - 1).wait()
        tile_in(i).start()

    @pl.when(k == nk - 1)
    def _():
        tile_in(i).wait()
        hbuf[...] = finish(hbuf[...])
        tile_out(i).start()

        @pl.when(i == nt - 1)
        def _():
            tile_out(i).wait()


def matmul_residual(a, w, layer, h, g_post, g_next, *, glu=False, tm=512, tk=512, manual_h=False):
    a_list = list(a) if isinstance(a, (tuple, list)) else [a]
    n_a = len(a_list)
    m = a_list[0].shape[0]
    kdim, n = w.shape[1:]
    d = h.shape[1]
    if n_a == 1:
        a_specs = [pl.BlockSpec((tm, tk), lambda i, k: (i, k))]
    else:
        tk = kdim
        assert sum(x.shape[1] for x in a_list) == kdim
        a_specs = [pl.BlockSpec((tm, x.shape[1]), lambda i, k: (i, 0)) for x in a_list]
    nk = kdim // tk
    assert nk >= 2 or not manual_h
    kern = functools.partial(_mm_res_kernel, nk=nk, glu=glu, d=d, n_a=n_a, tm=tm, manual_h=manual_h)
    h_spec = pl.BlockSpec(memory_space=pl.ANY) if manual_h else pl.BlockSpec((tm, d), lambda i, k: (i, 0))
    scratch = [pltpu.VMEM((tm, n), F32)]
    if manual_h:
        scratch += [pltpu.VMEM((tm, d), F32), pltpu.SemaphoreType.DMA((2,))]
    return pl.pallas_call(
        kern,
        grid=(m // tm, nk),
        in_specs=a_specs + [
                  pl.BlockSpec((None, tk, n), lambda i, k: (layer, k, 0)),
                  h_spec,
                  pl.BlockSpec((1, d), lambda i, k: (0, 0)),
                  pl.BlockSpec((1, d), lambda i, k: (0, 0))],
        out_specs=[h_spec, pl.BlockSpec((tm, d), lambda i, k: (i, 0))],
        out_shape=[jax.ShapeDtypeStruct((m, d), F32),
                   jax.ShapeDtypeStruct((m, d), BF16)],
        scratch_shapes=scratch,
        compiler_params=_cparams(("arbitrary" if manual_h else "parallel", "arbitrary")),
        name="matmul_residual",
    )(*a_list, w, h, g_post.reshape(1, d), g_next.reshape(1, d))


def _sb_kernel(q_ref, k_ref, v_ref, u_ref, o_ref, acc_ref, c_ref, *, tq, tk, ts, scale):
    i = pl.program_id(2)
    q = (q_ref[...].astype(F32) * scale).astype(BF16)
    u = u_ref[...]
    acc_ref[...] = jnp.zeros_like(acc_ref)
    c_ref[...] = jnp.zeros_like(c_ref)

    def run(items):
        def rows(ref, it):
            return ref[pl.ds(pl.multiple_of(it[0] * tk, tk), tk), :]

        def logits(it):
            return _dot_nt(rows(k_ref, it), q[it[2], :])

        def keep_mask(it, shape):
            spos = it[1] + lax.broadcasted_iota(jnp.int32, shape, 0)
            tpos = it[2].start + lax.broadcasted_iota(jnp.int32, shape, 1)
            return spos < tpos

        def suffix(it, z):
            ls = jnp.minimum(z, 0.0) - jnp.log(1.0 + jnp.exp2(jnp.abs(z) * (-LOG2E)))
            lk = ls - z
            if it[1] is not None:
                lk = jnp.where(keep_mask(it, z.shape), lk, 0.0)
            later = _dot(u, lk.astype(BF16))
            return ls, later, later[0:1, :] + lk[0:1, :]

        def weighted(it, ls, later):
            w = jnp.exp(ls + later)
            if it[1] is not None:
                w = jnp.where(keep_mask(it, w.shape), w, 0.0)
            return _dot_tn(rows(v_ref, it), w.astype(BF16))

        zs = [logits(it) for it in items]
        sfx = [None] * len(items)
        pvs = [None] * len(items)
        sfx[0] = suffix(items[0], zs[0])
        for n in range(1, len(items)):
            sfx[n] = suffix(items[n], zs[n])
            pvs[n - 1] = weighted(items[n - 1], sfx[n - 1][0], sfx[n - 1][1])
        pvs[-1] = weighted(items[-1], sfx[-1][0], sfx[-1][1])
        for it, (_, _, total), pv in zip(items, sfx, pvs):
            acc_ref[:, it[2]] += jnp.exp(c_ref[:, it[2]]) * pv
            c_ref[:, it[2]] += total

    nstraddle = tq // tk
    items = []
    for s in range(nstraddle - 1, -1, -1):
        for j in range(tq // ts):
            k_lo, q_lo = s * tk, j * ts
            if k_lo >= q_lo + ts - 1:
                continue
            items.append((i * nstraddle + s, k_lo if k_lo + tk > q_lo else None,
                          slice(q_lo, q_lo + ts)))
    run(items)

    def full_blocks(first, count):
        run([(first - n, None, slice(0, tq)) for n in range(count)])

    nfull = i * nstraddle
    group = 2 * nstraddle

    @pl.when(nfull % group != 0)
    def _():
        full_blocks(nfull - 1, nstraddle)

    def body(n, carry):
        full_blocks(nfull - nfull % group - 1 - group * n, group)
        return carry

    lax.fori_loop(0, nfull // group, body, 0)
    o_ref[...] = acc_ref[...].T.astype(o_ref.dtype)


def _sb_umat(tk):
    s = np.arange(tk)[:, None]
    j = np.arange(tk)[None, :]
    return jnp.asarray((j > s).astype(np.float32), dtype=BF16)


def sb_attention(proj, batch, seq, tq=SB_TQ, tk=SB_TK):
    d, nh = SB_HEAD_DIM, SB_HEADS
    assert (tq // tk) % 2 == 0
    nq = seq // tq
    kern = functools.partial(_sb_kernel, tq=tq, tk=tk, ts=SB_TS, scale=d ** -0.5)
    return pl.pallas_call(
        kern,
        grid=(batch, nh, nq),
        in_specs=[pl.BlockSpec((tq, d), lambda b, h, i: (b * nq + i, h)),
                  pl.BlockSpec((seq, d), lambda b, h, i: (b, nh + h)),
                  pl.BlockSpec((seq, d), lambda b, h, i: (b, 2 * nh + h)),
                  pl.BlockSpec((tk, tk), lambda b, h, i: (0, 0))],
        out_specs=pl.BlockSpec((tq, d), lambda b, h, i: (b * nq + i, h)),
        out_shape=jax.ShapeDtypeStruct((batch * seq, nh * d), BF16),
        scratch_shapes=[pltpu.VMEM((d, tq), F32), pltpu.VMEM((1, tq), F32)],
        compiler_params=_cparams(("parallel", "parallel", "parallel")),
        name="sb_attention",
    )(proj, proj, proj, _sb_umat(tk))


def _gla_levels(c):
    return int(math.log2(c))


def _gla_masks(c):
    t = np.arange(c)[:, None]
    s = np.arange(c)[None, :]
    masks = []
    for lev in range(_gla_levels(c)):
        half = c >> (lev + 1)
        blk = 2 * half
        masks.append((t // blk == s // blk) & ((t % blk) >= half) & ((s % blk) < half))
    masks.append(t == s)
    m = np.concatenate(masks, axis=0).astype(np.float32)
    return jnp.asarray(np.concatenate([m, m], axis=1))


def _gla_boundary_rows(cum, half):
    c, width = cum.shape
    blk = 2 * half
    sub = 8
    if half >= sub:
        return jnp.concatenate(
            [jnp.broadcast_to(cum[b * blk + half - 1:b * blk + half, :], (blk, width)) for b in range(c // blk)],
            axis=0)
    x = cum.reshape(c // sub, sub, width)
    row = lax.broadcasted_iota(jnp.int32, (c // sub, sub, width), 1)
    out = None
    for b in range(sub // blk):
        piece = jnp.broadcast_to(x[:, b * blk + half - 1:b * blk + half, :], x.shape)
        out = piece if out is None else jnp.where(row >= b * blk, piece, out)
    return out.reshape(c, width)


def _gla_kernel(q_ref, k_ref, v_ref, r_ref, glr_ref, wg_ref, bg_ref, gain_ref, t_ref, m_ref, bm_ref,
                o_ref, st_ref, *, c, nlev):
    @pl.when(pl.program_id(1) == 0)
    def _():
        st_ref[...] = jnp.zeros_like(st_ref)

    dk, dv = GLA_DK, GLA_DV
    npair = GLA_HEADS // 2
    logits = jnp.dot(glr_ref[...].astype(F32), wg_ref[...], precision=_HIGHEST,
                     preferred_element_type=F32) + bg_ref[...]
    la = jax.nn.log_sigmoid(logits) / GLA_GATE_TAU
    hi, lo = _split_bf16(la)
    cum = _dot(t_ref[...], hi) + _dot(t_ref[...], lo)
    total = cum[c - 1:c, :]
    f_in = jnp.exp(cum)
    f_out = jnp.exp(total - cum)
    f_lev = [jnp.exp(-jnp.abs(cum - _gla_boundary_rows(cum, c >> (lev + 1)))) for lev in range(nlev)]

    q = q_ref[...].astype(F32) * (dk ** -0.5)
    k = k_ref[...].astype(F32)
    qs = [(q * f).astype(BF16) for f in f_lev] + [q.astype(BF16)]
    ks = [(k * f).astype(BF16) for f in f_lev] + [k.astype(BF16)]
    q_in = (q * f_in).astype(BF16)
    k_out = (k * f_out).astype(BF16)
    ones = jnp.ones((c, 2 * dv), BF16)
    head_a = lax.broadcasted_iota(jnp.int32, (c, 2 * dk), 1) < dk
    zero_k = jnp.zeros((c, 2 * dk), BF16)
    zero_v = jnp.zeros((c, dv), BF16)

    for p in range(npair):
        lanes = slice(p * 2 * dk, (p + 1) * 2 * dk)
        wide = slice(p * 2 * dv, (p + 1) * 2 * dv)
        scores = jnp.zeros((c, 2 * c), F32)
        for lev in range(nlev + 1):
            kp = ks[lev][:, lanes]
            kstack = jnp.concatenate([jnp.where(head_a, kp, zero_k), jnp.where(head_a, zero_k, kp)], axis=0)
            scores = scores + m_ref[lev * c:(lev + 1) * c, :] * _dot_nt(qs[lev][:, lanes], kstack)
        v = v_ref[:, wide]
        v_bd = jnp.concatenate([jnp.concatenate([v[:, :dv], zero_v], axis=1),
                                jnp.concatenate([zero_v, v[:, dv:]], axis=1)], axis=0)
        st = st_ref[p]
        o = _dot(scores.astype(BF16), v_bd) + _dot(q_in[:, lanes], st.astype(BF16))
        tot_col = _dot_tn(hi[:, lanes], ones) + _dot_tn(lo[:, lanes], ones)
        st_ref[p] = jnp.exp(tot_col) * st + bm_ref[...] * _dot_tn(k_out[:, lanes], v)
        for hd in range(2):
            oh = o[:, hd * dv:(hd + 1) * dv]
            oh = oh * lax.rsqrt(jnp.mean(oh * oh, axis=-1, keepdims=True) + NORM_EPS)
            cols = slice(p * 2 * dv + hd * dv, p * 2 * dv + (hd + 1) * dv)
            rr = r_ref[:, cols].astype(F32)
            o_ref[:, cols] = (oh * gain_ref[:, cols] * (rr * jax.nn.sigmoid(rr))).astype(o_ref.dtype)


def gla_attention(proj, w_gate_pad, b_gate, gain, batch, seq, col0, c=GLA_CHUNK):
    dk, dv, nh = GLA_DK, GLA_DV, GLA_HEADS
    nc = seq // c
    nlev = _gla_levels(c)
    kw, vw = nh * dk, nh * dv
    qb = col0 // kw
    vb = (col0 + 2 * kw) // vw
    lrb = (col0 + 2 * kw + 2 * vw) // LANES
    tincl = jnp.asarray(np.tril(np.ones((c, c), np.float32)), dtype=BF16)
    pair_blocks = jnp.asarray(np.kron(np.eye(2, dtype=np.float32), np.ones((dk, dv), np.float32)))
    kern = functools.partial(_gla_kernel, c=c, nlev=nlev)
    row = lambda b, n: b * nc + n
    const = lambda shape: pl.BlockSpec(shape, lambda b, n: (0, 0))
    return pl.pallas_call(
        kern,
        grid=(batch, nc),
        in_specs=[pl.BlockSpec((c, kw), lambda b, n: (row(b, n), qb)),
                  pl.BlockSpec((c, kw), lambda b, n: (row(b, n), qb + 1)),
                  pl.BlockSpec((c, vw), lambda b, n: (row(b, n), vb)),
                  pl.BlockSpec((c, vw), lambda b, n: (row(b, n), vb + 1)),
                  pl.BlockSpec((c, LANES), lambda b, n: (row(b, n), lrb)),
                  const((LANES, kw)), const((1, kw)), const((1, vw)),
                  const((c, c)), const(((nlev + 1) * c, 2 * c)), const((2 * dk, 2 * dv))],
        out_specs=pl.BlockSpec((c, vw), lambda b, n: (row(b, n), 0)),
        out_shape=jax.ShapeDtypeStruct((batch * seq, vw), BF16),
        scratch_shapes=[pltpu.VMEM((nh // 2, 2 * dk, 2 * dv), F32)],
        compiler_params=_cparams(("parallel", "arbitrary")),
        name="gla",
    )(proj, proj, proj, proj, proj, w_gate_pad, b_gate.reshape(1, -1), gain.reshape(1, -1),
      tincl, _gla_masks(c), pair_blocks)


def _s5_prep_kernel(lr_ref, li_ref, ls_ref, bre_ref, bim_ref, cre_ref, cim_ref,
                    tin_ref, tout_ref, toep_ref, lam_ref, *, gb, r):
    h = S5_GROUP
    r16 = r * h
    first = lax.broadcasted_iota(jnp.int32, (1, LANES), 1) < S5_STATE
    npow = -(-(r + 1) // 8) * 8
    kidx = lax.broadcasted_iota(jnp.int32, (npow, LANES), 0).astype(F32)
    lane_group = (lax.broadcasted_iota(jnp.int32, (h, r * LANES), 1) % LANES) // h

    def tile_rows(x):
        return jnp.broadcast_to(x[None], (r, h, LANES)).reshape(r16, LANES)

    def by_token(x):
        return x.reshape(r, h, LANES).astype(BF16)

    def rows_of(table, powers):
        return jnp.concatenate([jnp.broadcast_to(table[k:k + 1, :], (h, LANES)) for k in powers], axis=0)

    lam_r, lam_i = [], []
    for gi in range(gb):
        lr = lr_ref[gi:gi + 1, :]
        li = li_ref[gi:gi + 1, :]
        dt = jnp.exp(ls_ref[gi:gi + 1, :])
        mag = jnp.exp(kidx * (dt * lr))
        ang = kidx * (dt * li)
        pw_r, pw_i = mag * jnp.cos(ang), mag * jnp.sin(ang)
        lbr, lbi = pw_r[1:2, :], pw_i[1:2, :]
        den = lr * lr + li * li
        nr, ni = lbr - 1.0, lbi
        cr = (nr * lr + ni * li) / den
        ci = (ni * lr - nr * li) / den
        bre, bim = bre_ref[gi], bim_ref[gi]
        bbr = cr * bre - ci * bim
        bbi = cr * bim + ci * bre
        pr, pi = rows_of(pw_r, range(r + 1)), rows_of(pw_i, range(r + 1))
        prr, pir = rows_of(pw_r, range(r - 1, -1, -1)), rows_of(pw_i, range(r - 1, -1, -1))
        bbr_t, bbi_t = tile_rows(bbr), tile_rows(bbi)
        tin_ref[0, :, gi] = by_token(prr * bbr_t - pir * bbi_t)
        tin_ref[1, :, gi] = by_token(prr * bbi_t + pir * bbr_t)
        cr_t, ci_t = tile_rows(cre_ref[gi]), tile_rows(cim_ref[gi])
        pr1, pi1 = pr[h:], pi[h:]
        tout_ref[0, :, gi] = by_token(cr_t * pr1 - ci_t * pi1)
        tout_ref[1, :, gi] = by_token(-(cr_t * pi1 + ci_t * pr1))
        pr0, pi0 = pr[:r16], pi[:r16]
        wk = jnp.where(first, cr_t * pr0 - ci_t * pi0, -(cr_t * pi0 + ci_t * pr0))
        wk_t = jnp.broadcast_to(wk.reshape(r, 1, h, LANES), (r, gb, h, LANES)).reshape(r * gb * h, LANES)
        bb = jnp.where(first, bbr, bbi)
        mt = _dot_nt(bb.astype(BF16), wk_t.astype(BF16))
        toep_ref[gi * h:(gi + 1) * h, :] = jnp.where(lane_group == gi, mt, 0.0).astype(toep_ref.dtype)
        lam_r.append(pr[r * h:r * h + 1, :])
        lam_i.append(pi[r * h:r * h + 1, :])

    def pairs(rows):
        return jnp.concatenate([jnp.where(first, rows[2 * q], rows[2 * q + 1]) for q in range(gb // 2)], axis=1)

    lam_ref[0:1, :] = pairs(lam_r)
    lam_ref[1:2, :] = pairs(lam_i)


def s5_prep(lam_re, lam_im, log_step, b_re, b_im, c_re, c_im, r=S5_R, gb=S5_GB):
    g = lam_re.shape[0]
    h = S5_GROUP
    nlb = g // gb
    dbl = lambda x: jnp.concatenate([x, x], axis=-1)
    lr2, li2 = dbl(lam_re), dbl(lam_im)
    ls2 = jnp.broadcast_to(log_step[:, None], (g, LANES))
    bre2 = dbl(jnp.swapaxes(b_re, 1, 2))
    bim2 = dbl(jnp.swapaxes(b_im, 1, 2))
    cre2, cim2 = dbl(c_re), dbl(c_im)
    vec = pl.BlockSpec((gb, LANES), lambda i: (i, 0))
    mat = pl.BlockSpec((gb, h, LANES), lambda i: (i, 0, 0))
    tspec = pl.BlockSpec((None, 2, r, gb, h, LANES), lambda i: (i, 0, 0, 0, 0, 0))
    tshape = jax.ShapeDtypeStruct((nlb, 2, r, gb, h, LANES), BF16)
    kern = functools.partial(_s5_prep_kernel, gb=gb, r=r)
    tin, tout, toep, lam = pl.pallas_call(
        kern,
        grid=(nlb,),
        in_specs=[vec, vec, vec, mat, mat, mat, mat],
        out_specs=[tspec, tspec,
                   pl.BlockSpec((None, gb * h, r * LANES), lambda i: (i, 0, 0)),
                   pl.BlockSpec((None, 2, gb // 2 * LANES), lambda i: (i, 0, 0))],
        out_shape=[tshape, tshape,
                   jax.ShapeDtypeStruct((nlb, gb * h, r * LANES), BF16),
                   jax.ShapeDtypeStruct((nlb, 2, gb // 2 * LANES), F32)],
        compiler_params=_cparams(("parallel",)),
        name="s5_prep",
    )(lr2, li2, ls2, bre2, bim2, cre2, cim2)
    rows = r * gb * h
    return tin.reshape(nlb, 2, rows, LANES), tout.reshape(nlb, 2, rows, LANES), toep, lam


def _s5_pair_mask(r, gb):
    group = (np.arange(r * LANES) % LANES) // S5_GROUP
    lane = np.arange(gb // 2 * LANES)
    target = 2 * (lane // LANES) + (lane % LANES) // S5_STATE
    return jnp.asarray((group[:, None] == target[None, :]).astype(np.float32), dtype=BF16)


def _s5_kernel(a_ref, tin_ref, tout_ref, toep_ref, lam_ref, mask_ref, d_ref, y_ref,
               a32_scr, y32_scr, bdin_scr, bdout_scr, bdt_scr, s_scr, xp_scr, *, batch, jn, r, gb):
    bj = batch * jn
    npair = gb // 2
    half = npair * LANES
    a32_scr[...] = a_ref[...].astype(F32)
    a_tok = [a32_scr[pl.ds(i, bj, stride=r), :] for i in range(r)]

    for q in range(npair):
        mq = mask_ref[:, q * LANES:(q + 1) * LANES]
        for part in range(2):
            cols = slice((part * npair + q) * LANES, (part * npair + q + 1) * LANES)
            bdin_scr[:, cols] = tin_ref[part] * mq
            bdout_scr[:, cols] = tout_ref[part] * mq
    for i in range(r):
        if i > 0:
            bdt_scr[i * LANES:(i + 1) * LANES, :i * LANES] = jnp.zeros((LANES, i * LANES), bdt_scr.dtype)
        bdt_scr[i * LANES:(i + 1) * LANES, i * LANES:] = toep_ref[:, :(r - i) * LANES]

    a = jnp.concatenate([x.astype(BF16) for x in a_tok], axis=1)
    s_scr[...] = _dot(a, bdin_scr[...])
    lam_r = lam_ref[0:1, :]
    lam_i = lam_ref[1:2, :]

    def step(j, carry):
        out = []
        for b in range(batch):
            xr, xi = carry[2 * b], carry[2 * b + 1]
            row = pl.ds(b * jn + j, 1)
            xp_scr[row, :half] = xr
            xp_scr[row, half:] = xi
            s = s_scr[row, :]
            out.append(lam_r * xr - lam_i * xi + s[:, :half])
            out.append(lam_r * xi + lam_i * xr + s[:, half:])
        return tuple(out)

    zero = jnp.zeros((1, half), F32)
    lax.fori_loop(0, jn, step, (zero,) * (2 * batch))

    y_state = _dot_nt(xp_scr[...].astype(BF16), bdout_scr[...])
    d = d_ref[...]
    span = r // S5_TOEP_SPLIT
    for g in range(S5_TOEP_SPLIT):
        rows = (g + 1) * span * LANES
        cols = slice(g * span * LANES, rows)
        y = y_state[:, cols] + _dot(a[:, :rows], bdt_scr[:rows, cols])
        for i in range(g * span, (g + 1) * span):
            yi = y[:, (i - g * span) * LANES:(i - g * span + 1) * LANES] + a_tok[i] * d
            y32_scr[pl.ds(i, bj, stride=r), :] = 0.5 * yi * (1.0 + lax.erf(yi * (2.0 ** -0.5)))
    y_ref[...] = y32_scr[...].astype(y_ref.dtype)


def s5_mixer_gelu(a_norm, batch, seq, lam_re, lam_im, log_step, b_re, b_im, c_re, c_im, d_skip,
                  r=S5_R, gb=S5_GB):
    m, d = a_norm.shape
    assert gb * S5_GROUP == LANES
    nlb = d // LANES
    jn = seq // r
    bj = batch * jn
    half = gb // 2 * LANES
    tin, tout, toep, lam = s5_prep(lam_re, lam_im, log_step, b_re, b_im, c_re, c_im, r=r, gb=gb)
    lane_block = pl.BlockSpec((m, LANES), lambda lb: (0, lb))
    whole = lambda *shape: pl.BlockSpec((None,) + shape, lambda lb: (lb,) + (0,) * len(shape))
    kern = functools.partial(_s5_kernel, batch=batch, jn=jn, r=r, gb=gb)
    return pl.pallas_call(
        kern,
        grid=(nlb,),
        in_specs=[lane_block,
                  whole(2, r * LANES, LANES), whole(2, r * LANES, LANES), whole(LANES, r * LANES), whole(2, half),
                  pl.BlockSpec((r * LANES, half), lambda lb: (0, 0)),
                  pl.BlockSpec((1, LANES), lambda lb: (0, lb))],
        out_specs=lane_block,
        out_shape=jax.ShapeDtypeStruct((m, d), BF16),
        scratch_shapes=[pltpu.VMEM((m, LANES), F32), pltpu.VMEM((m, LANES), F32),
                        pltpu.VMEM((r * LANES, 2 * half), BF16), pltpu.VMEM((r * LANES, 2 * half), BF16),
                        pltpu.VMEM((r * LANES, r * LANES), BF16),
                        pltpu.VMEM((bj, 2 * half), F32), pltpu.VMEM((bj, 2 * half), F32)],
        compiler_params=_cparams(("parallel",)),
        name="s5_blocks",
    )(a_norm, tin, tout, toep, lam, _s5_pair_mask(r, gb), d_skip.reshape(1, d))


def kernel(x, norm_gains, w_in, w_gate_up, b_gate, gla_norm_gain, w_out, s5_lambda_re, s5_lambda_im,
           s5_log_step, s5_b_re, s5_b_im, s5_c_re, s5_c_im, s5_d, w_glu, w_ffn_in, w_ffn_out):
    batch, seq, d = x.shape
    m = batch * seq
    depth = norm_gains.shape[0]
    sb_w = SB_HEADS * SB_HEAD_DIM

    proj_w = -(-w_in.shape[2] // PROJ_TN) * PROJ_TN
    w_in_b = jnp.pad(w_in, ((0, 0), (0, 0), (0, proj_w - w_in.shape[2]))).astype(BF16)
    w_out_b, w_glu_b, w_ffn_out_b = w_out.astype(BF16), w_glu.astype(BF16), w_ffn_out.astype(BF16)

    h = x.astype(F32).reshape(m, d)
    a = rms_norm_bf16(h, norm_gains[0, 0])
    for layer in range(depth):
        gains = norm_gains[layer]
        i = layer // 2
        if layer % 2 == 0:
            proj = matmul(a, w_in_b, i, proj_w, BF16, tm=1024, tn=PROJ_TN)
            o_sb = sb_attention(proj, batch, seq)
            w_gate_pad = jnp.pad(w_gate_up[i], ((0, LANES - GLA_GATE_RANK), (0, 0)))
            o_gla = gla_attention(proj, w_gate_pad, b_gate[i], gla_norm_gain[i], batch, seq, 3 * sb_w)
            h, a = matmul_residual((o_sb, o_gla), w_out_b, i, h, gains[1], gains[2])
        else:
            y = s5_mixer_gelu(a, batch, seq, s5_lambda_re[i], s5_lambda_im[i], s5_log_step[i],
                              s5_b_re[i], s5_b_im[i], s5_c_re[i], s5_c_im[i], s5_d[i])
            h, a = matmul_residual(y, w_glu_b, i, h, gains[1], gains[2], glu=True, tm=1024, tk=d // 4,
                                   manual_h=True)
        f = ffn_in(a, w_ffn_in, layer)
        g_next = norm_gains[layer + 1, 0] if layer + 1 < depth else gains[3]
        h, a = matmul_residual(f, w_ffn_out_b, layer, h, gains[3], g_next, tm=1024,
                               tk=w_ffn_out.shape[1] // 4, manual_h=True)
    return h.reshape(batch, seq, d).astype(x.dtype)
```

```python
import functools
import math

import numpy as np
import jax
import jax.numpy as jnp
from jax import lax
from jax.experimental import pallas as pl
from jax.experimental.pallas import tpu as pltpu

F32 = jnp.float32
BF16 = jnp.bfloat16

NORM_EPS = 1e-6
LANES = 128
SB_HEADS = 8
SB_HEAD_DIM = 128
GLA_HEADS = 8
GLA_DK = 64
GLA_DV = 128
GLA_GATE_RANK = 16
GLA_GATE_TAU = 16.0
S5_GROUP = 16
S5_STATE = 64

SB_TQ = 1024
SB_TK = 256
GLA_CHUNK = 128
PROJ_TN = 1280
S5_R = 16
S5_GB = 8
S5_TOEP_SPLIT = 4

VMEM_LIMIT = 56 * 1024 * 1024

_HIGHEST = lax.Precision.HIGHEST
LOG2E = math.log2(math.e)


def _cparams(sem):
    return pltpu.CompilerParams(dimension_semantics=sem, vmem_limit_bytes=VMEM_LIMIT)


def _rms(x, g):
    return x * lax.rsqrt(jnp.mean(x * x, axis=-1, keepdims=True) + NORM_EPS) * g


def _dot(a, b):
    return jnp.dot(a, b, preferred_element_type=F32)


def _dot_nt(a, b):
    return lax.dot_general(a, b, (((1,), (1,)), ((), ())), preferred_element_type=F32)


def _dot_tn(a, b):
    return lax.dot_general(a, b, (((0,), (0,)), ((), ())), preferred_element_type=F32)


def _split_bf16(x):
    hi = x.astype(BF16)
    lo = (x - hi.astype(F32)).astype(BF16)
    return hi, lo


def _norm_kernel(h_ref, g_ref, o_ref):
    o_ref[...] = _rms(h_ref[...], g_ref[...]).astype(o_ref.dtype)


def rms_norm_bf16(h, g, tm=512):
    m, d = h.shape
    return pl.pallas_call(
        _norm_kernel,
        grid=(m // tm,),
        in_specs=[pl.BlockSpec((tm, d), lambda i: (i, 0)),
                  pl.BlockSpec((1, d), lambda i: (0, 0))],
        out_specs=pl.BlockSpec((tm, d), lambda i: (i, 0)),
        out_shape=jax.ShapeDtypeStruct((m, d), BF16),
        compiler_params=_cparams(("parallel",)),
        name="rms_norm",
    )(h, g.reshape(1, d))


def _mm_kernel(a_ref, w_ref, o_ref, wb_ref):
    @pl.when(pl.program_id(1) == 0)
    def _():
        wb_ref[...] = w_ref[...].astype(BF16)

    o_ref[...] = _dot(a_ref[...], wb_ref[...]).astype(o_ref.dtype)


def matmul(a, w, layer, ncols, out_dtype, tm, tn):
    m, k = a.shape
    return pl.pallas_call(
        _mm_kernel,
        grid=(ncols // tn, m // tm),
        in_specs=[pl.BlockSpec((tm, k), lambda j, i: (i, 0)),
                  pl.BlockSpec((None, k, tn), lambda j, i: (layer, 0, j))],
        out_specs=pl.BlockSpec((tm, tn), lambda j, i: (i, j)),
        out_shape=jax.ShapeDtypeStruct((m, ncols), out_dtype),
        scratch_shapes=[pltpu.VMEM((k, tn), BF16)],
        compiler_params=_cparams(("parallel", "arbitrary")),
        name="matmul",
    )(a, w)


def _ffn_in_kernel(a_ref, wg_ref, wu_ref, o_ref, wgb_ref, wub_ref):
    @pl.when(pl.program_id(1) == 0)
    def _():
        wgb_ref[...] = wg_ref[...].astype(BF16)
        wub_ref[...] = wu_ref[...].astype(BF16)

    a = a_ref[...]
    g = _dot(a, wgb_ref[...])
    u = _dot(a, wub_ref[...])
    o_ref[...] = (g * jax.nn.sigmoid(g) * u).astype(o_ref.dtype)


def ffn_in(a, w, layer, tm=2048, tn=512):
    m, k = a.shape
    nf = w.shape[2] // 2
    nj = nf // tn
    return pl.pallas_call(
        _ffn_in_kernel,
        grid=(nj, m // tm),
        in_specs=[pl.BlockSpec((tm, k), lambda j, i: (i, 0)),
                  pl.BlockSpec((None, k, tn), lambda j, i: (layer, 0, j)),
                  pl.BlockSpec((None, k, tn), lambda j, i: (layer, 0, j + nj))],
        out_specs=pl.BlockSpec((tm, tn), lambda j, i: (i, j)),
        out_shape=jax.ShapeDtypeStruct((m, nf), BF16),
        scratch_shapes=[pltpu.VMEM((k, tn), BF16), pltpu.VMEM((k, tn), BF16)],
        compiler_params=_cparams(("parallel", "arbitrary")),
        name="ffn_in",
    )(a, w, w)


def _mm_res_kernel(*refs, nk, glu, d, n_a, tm, manual_h):
    a_refs = refs[:n_a]
    w_ref, h_ref, gpost_ref, gnext_ref, hout_ref, anext_ref, acc_ref = refs[n_a:n_a + 7]
    i = pl.program_id(0)
    k = pl.program_id(1)

    def product():
        if n_a == 1:
            return _dot(a_refs[0][...], w_ref[...])
        out, lo = None, 0
        for ar in a_refs:
            part = _dot(ar[...], w_ref[lo:lo + ar.shape[1], :])
            out = part if out is None else out + part
            lo += ar.shape[1]
        return out

    def finish(h_tile):
        y = acc_ref[...]
        if glu:
            y = y[:, :d] * jax.nn.sigmoid(y[:, d:])
        hn = h_tile + _rms(y, gpost_ref[...])
        anext_ref[...] = _rms(hn, gnext_ref[...]).astype(anext_ref.dtype)
        return hn

    @pl.when(k == 0)
    def _():
        acc_ref[...] = product()

    @pl.when(k > 0)
    def _():
        acc_ref[...] += product()

    if not manual_h:
        @pl.when(k == nk - 1)
        def _():
            hout_ref[...] = finish(h_ref[...])
        return

    hbuf, sems = refs[n_a + 7:]
    nt = pl.num_programs(0)

    def tile_in(t):
        return pltpu.make_async_copy(h_ref.at[pl.ds(pl.multiple_of(t * tm, tm), tm), :], hbuf, sems.at[0])

    def tile_out(t):
        return pltpu.make_async_copy(hbuf, hout_ref.at[pl.ds(pl.multiple_of(t * tm, tm), tm), :], sems.at[1])

    @pl.when(k == 1)
    def _():
        @pl.when(i > 0)
        def _():
            tile_out(i - 1).wait()
        tile_in(i).start()

    @pl.when(k == nk - 1)
    def _():
        tile_in(i).wait()
        hbuf[...] = finish(hbuf[...])
        tile_out(i).start()

        @pl.when(i == nt - 1)
        def _():
            tile_out(i).wait()


def matmul_residual(a, w, layer, h, g_post, g_next, *, glu=False, tm=512, tk=512, manual_h=False):
    a_list = list(a) if isinstance(a, (tuple, list)) else [a]
    n_a = len(a_list)
    m = a_list[0].shape[0]
    kdim, n = w.shape[1:]
    d = h.shape[1]
    if n_a == 1:
        a_specs = [pl.BlockSpec((tm, tk), lambda i, k: (i, k))]
    else:
        tk = kdim
        assert sum(x.shape[1] for x in a_list) == kdim
        a_specs = [pl.BlockSpec((tm, x.shape[1]), lambda i, k: (i, 0)) for x in a_list]
    nk = kdim // tk
    assert nk >= 2 or not manual_h
    kern = functools.partial(_mm_res_kernel, nk=nk, glu=glu, d=d, n_a=n_a, tm=tm, manual_h=manual_h)
    h_spec = pl.BlockSpec(memory_space=pl.ANY) if manual_h else pl.BlockSpec((tm, d), lambda i, k: (i, 0))
    scratch = [pltpu.VMEM((tm, n), F32)]
    if manual_h:
        scratch += [pltpu.VMEM((tm, d), F32), pltpu.SemaphoreType.DMA((2,))]
    return pl.pallas_call(
        kern,
        grid=(m // tm, nk),
        in_specs=a_specs + [
                  pl.BlockSpec((None, tk, n), lambda i, k: (layer, k, 0)),
                  h_spec,
                  pl.BlockSpec((1, d), lambda i, k: (0, 0)),
                  pl.BlockSpec((1, d), lambda i, k: (0, 0))],
        out_specs=[h_spec, pl.BlockSpec((tm, d), lambda i, k: (i, 0))],
        out_shape=[jax.ShapeDtypeStruct((m, d), F32),
                   jax.ShapeDtypeStruct((m, d), BF16)],
        scratch_shapes=scratch,
        compiler_params=_cparams(("arbitrary" if manual_h else "parallel", "arbitrary")),
        name="matmul_residual",
    )(*a_list, w, h, g_post.reshape(1, d), g_next.reshape(1, d))


def _sb_kernel(q_ref, k_ref, v_ref, u_ref, o_ref, acc_ref, c_ref, *, tq, tk, scale):
    i = pl.program_id(2)
    q = (q_ref[...].astype(F32) * scale).astype(BF16)
    u = u_ref[...]
    acc_ref[...] = jnp.zeros_like(acc_ref)
    c_ref[...] = jnp.zeros_like(c_ref)

    def run(items):
        def rows(ref, it):
            return ref[pl.ds(pl.multiple_of(it[0] * tk, tk), tk), :]

        def logits(it):
            return _dot_nt(rows(k_ref, it), q[it[2], :])

        def keep_mask(it, shape):
            spos = it[1] + lax.broadcasted_iota(jnp.int32, shape, 0)
            tpos = it[2].start + lax.broadcasted_iota(jnp.int32, shape, 1)
            return spos < tpos

        def suffix(it, z):
            ls = jnp.minimum(z, 0.0) - jnp.log(1.0 + jnp.exp2(jnp.abs(z) * (-LOG2E)))
            lk = ls - z
            if it[1] is not None:
                lk = jnp.where(keep_mask(it, z.shape), lk, 0.0)
            later = _dot(u, lk.astype(BF16))
            return ls, later, later[0:1, :] + lk[0:1, :]

        def weighted(it, ls, later):
            w = jnp.exp(ls + later)
            if it[1] is not None:
                w = jnp.where(keep_mask(it, w.shape), w, 0.0)
            return _dot_tn(rows(v_ref, it), w.astype(BF16))

        zs = [logits(it) for it in items]
        sfx = [None] * len(items)
        pvs = [None] * len(items)
        sfx[0] = suffix(items[0], zs[0])
        for n in range(1, len(items)):
            sfx[n] = suffix(items[n], zs[n])
            pvs[n - 1] = weighted(items[n - 1], sfx[n - 1][0], sfx[n - 1][1])
        pvs[-1] = weighted(items[-1], sfx[-1][0], sfx[-1][1])
        for it, (_, _, total), pv in zip(items, sfx, pvs):
            acc_ref[:, it[2]] += jnp.exp(c_ref[:, it[2]]) * pv
            c_ref[:, it[2]] += total

    nstraddle = tq // tk
    items = []
    for s in range(nstraddle - 1, -1, -1):
        items.append((i * nstraddle + s, s * tk, slice(s * tk, (s + 1) * tk)))
        if s + 1 < nstraddle:
            items.append((i * nstraddle + s, None, slice((s + 1) * tk, tq)))
    run(items)

    def full_blocks(first, count):
        run([(first - n, None, slice(0, tq)) for n in range(count)])

    nfull = i * nstraddle
    group = 2 * nstraddle

    @pl.when(nfull % group != 0)
    def _():
        full_blocks(nfull - 1, nstraddle)

    def body(n, carry):
        full_blocks(nfull - nfull % group - 1 - group * n, group)
        return carry

    lax.fori_loop(0, nfull // group, body, 0)
    o_ref[...] = acc_ref[...].T.astype(o_ref.dtype)


def _sb_umat(tk):
    s = np.arange(tk)[:, None]
    j = np.arange(tk)[None, :]
    return jnp.asarray((j > s).astype(np.float32), dtype=BF16)


def sb_attention(proj, batch, seq, tq=SB_TQ, tk=SB_TK):
    d, nh = SB_HEAD_DIM, SB_HEADS
    assert (tq // tk) % 2 == 0
    nq = seq // tq
    kern = functools.partial(_sb_kernel, tq=tq, tk=tk, scale=d ** -0.5)
    return pl.pallas_call(
        kern,
        grid=(batch, nh, nq),
        in_specs=[pl.BlockSpec((tq, d), lambda b, h, i: (b * nq + i, h)),
                  pl.BlockSpec((seq, d), lambda b, h, i: (b, nh + h)),
                  pl.BlockSpec((seq, d), lambda b, h, i: (b, 2 * nh + h)),
                  pl.BlockSpec((tk, tk), lambda b, h, i: (0, 0))],
        out_specs=pl.BlockSpec((tq, d), lambda b, h, i: (b * nq + i, h)),
        out_shape=jax.ShapeDtypeStruct((batch * seq, nh * d), BF16),
        scratch_shapes=[pltpu.VMEM((d, tq), F32), pltpu.VMEM((1, tq), F32)],
        compiler_params=_cparams(("parallel", "parallel", "parallel")),
        name="sb_attention",
    )(proj, proj, proj, _sb_umat(tk))


def _gla_levels(c):
    return int(math.log2(c))


def _gla_masks(c):
    t = np.arange(c)[:, None]
    s = np.arange(c)[None, :]
    masks = []
    for lev in range(_gla_levels(c)):
        half = c >> (lev + 1)
        blk = 2 * half
        masks.append((t // blk == s // blk) & ((t % blk) >= half) & ((s % blk) < half))
    masks.append(t == s)
    m = np.concatenate(masks, axis=0).astype(np.float32)
    return jnp.asarray(np.concatenate([m, m], axis=1))


def _gla_boundary_rows(cum, half):
    c, width = cum.shape
    blk = 2 * half
    sub = 8
    if half >= sub:
        return jnp.concatenate(
            [jnp.broadcast_to(cum[b * blk + half - 1:b * blk + half, :], (blk, width)) for b in range(c // blk)],
            axis=0)
    x = cum.reshape(c // sub, sub, width)
    row = lax.broadcasted_iota(jnp.int32, (c // sub, sub, width), 1)
    out = None
    for b in range(sub // blk):
        piece = jnp.broadcast_to(x[:, b * blk + half - 1:b * blk + half, :], x.shape)
        out = piece if out is None else jnp.where(row >= b * blk, piece, out)
    return out.reshape(c, width)


def _gla_kernel(q_ref, k_ref, v_ref, r_ref, glr_ref, wg_ref, bg_ref, gain_ref, t_ref, m_ref, bm_ref,
                o_ref, st_ref, *, c, nlev):
    @pl.when(pl.program_id(1) == 0)
    def _():
        st_ref[...] = jnp.zeros_like(st_ref)

    dk, dv = GLA_DK, GLA_DV
    npair = GLA_HEADS // 2
    logits = jnp.dot(glr_ref[...].astype(F32), wg_ref[...], precision=_HIGHEST,
                     preferred_element_type=F32) + bg_ref[...]
    la = jax.nn.log_sigmoid(logits) / GLA_GATE_TAU
    hi, lo = _split_bf16(la)
    cum = _dot(t_ref[...], hi) + _dot(t_ref[...], lo)
    total = cum[c - 1:c, :]
    f_in = jnp.exp(cum)
    f_out = jnp.exp(total - cum)
    f_lev = [jnp.exp(-jnp.abs(cum - _gla_boundary_rows(cum, c >> (lev + 1)))) for lev in range(nlev)]

    q = q_ref[...].astype(F32) * (dk ** -0.5)
    k = k_ref[...].astype(F32)
    qs = [(q * f).astype(BF16) for f in f_lev] + [q.astype(BF16)]
    ks = [(k * f).astype(BF16) for f in f_lev] + [k.astype(BF16)]
    q_in = (q * f_in).astype(BF16)
    k_out = (k * f_out).astype(BF16)
    ones = jnp.ones((c, 2 * dv), BF16)
    head_a = lax.broadcasted_iota(jnp.int32, (c, 2 * dk), 1) < dk
    zero_k = jnp.zeros((c, 2 * dk), BF16)
    zero_v = jnp.zeros((c, dv), BF16)

    for p in range(npair):
        lanes = slice(p * 2 * dk, (p + 1) * 2 * dk)
        wide = slice(p * 2 * dv, (p + 1) * 2 * dv)
        scores = jnp.zeros((c, 2 * c), F32)
        for lev in range(nlev + 1):
            kp = ks[lev][:, lanes]
            kstack = jnp.concatenate([jnp.where(head_a, kp, zero_k), jnp.where(head_a, zero_k, kp)], axis=0)
            scores = scores + m_ref[lev * c:(lev + 1) * c, :] * _dot_nt(qs[lev][:, lanes], kstack)
        v = v_ref[:, wide]
        v_bd = jnp.concatenate([jnp.concatenate([v[:, :dv], zero_v], axis=1),
                                jnp.concatenate([zero_v, v[:, dv:]], axis=1)], axis=0)
        st = st_ref[p]
        o = _dot(scores.astype(BF16), v_bd) + _dot(q_in[:, lanes], st.astype(BF16))
        tot_col = _dot_tn(hi[:, lanes], ones) + _dot_tn(lo[:, lanes], ones)
        st_ref[p] = jnp.exp(tot_col) * st + bm_ref[...] * _dot_tn(k_out[:, lanes], v)
        for hd in range(2):
            oh = o[:, hd * dv:(hd + 1) * dv]
            oh = oh * lax.rsqrt(jnp.mean(oh * oh, axis=-1, keepdims=True) + NORM_EPS)
            cols = slice(p * 2 * dv + hd * dv, p * 2 * dv + (hd + 1) * dv)
            rr = r_ref[:, cols].astype(F32)
            o_ref[:, cols] = (oh * gain_ref[:, cols] * (rr * jax.nn.sigmoid(rr))).astype(o_ref.dtype)


def gla_attention(proj, w_gate_pad, b_gate, gain, batch, seq, col0, c=GLA_CHUNK):
    dk, dv, nh = GLA_DK, GLA_DV, GLA_HEADS
    nc = seq // c
    nlev = _gla_levels(c)
    kw, vw = nh * dk, nh * dv
    qb = col0 // kw
    vb = (col0 + 2 * kw) // vw
    lrb = (col0 + 2 * kw + 2 * vw) // LANES
    tincl = jnp.asarray(np.tril(np.ones((c, c), np.float32)), dtype=BF16)
    pair_blocks = jnp.asarray(np.kron(np.eye(2, dtype=np.float32), np.ones((dk, dv), np.float32)))
    kern = functools.partial(_gla_kernel, c=c, nlev=nlev)
    row = lambda b, n: b * nc + n
    const = lambda shape: pl.BlockSpec(shape, lambda b, n: (0, 0))
    return pl.pallas_call(
        kern,
        grid=(batch, nc),
        in_specs=[pl.BlockSpec((c, kw), lambda b, n: (row(b, n), qb)),
                  pl.BlockSpec((c, kw), lambda b, n: (row(b, n), qb + 1)),
                  pl.BlockSpec((c, vw), lambda b, n: (row(b, n), vb)),
                  pl.BlockSpec((c, vw), lambda b, n: (row(b, n), vb + 1)),
                  pl.BlockSpec((c, LANES), lambda b, n: (row(b, n), lrb)),
                  const((LANES, kw)), const((1, kw)), const((1, vw)),
                  const((c, c)), const(((nlev + 1) * c, 2 * c)), const((2 * dk, 2 * dv))],
        out_specs=pl.BlockSpec((c, vw), lambda b, n: (row(b, n), 0)),
        out_shape=jax.ShapeDtypeStruct((batch * seq, vw), BF16),
        scratch_shapes=[pltpu.VMEM((nh // 2, 2 * dk, 2 * dv), F32)],
        compiler_params=_cparams(("parallel", "arbitrary")),
        name="gla",
    )(proj, proj, proj, proj, proj, w_gate_pad, b_gate.reshape(1, -1), gain.reshape(1, -1),
      tincl, _gla_masks(c), pair_blocks)


def _s5_prep_kernel(lr_ref, li_ref, ls_ref, bre_ref, bim_ref, cre_ref, cim_ref,
                    tin_ref, tout_ref, toep_ref, lam_ref, *, gb, r):
    h = S5_GROUP
    r16 = r * h
    first = lax.broadcasted_iota(jnp.int32, (1, LANES), 1) < S5_STATE
    npow = -(-(r + 1) // 8) * 8
    kidx = lax.broadcasted_iota(jnp.int32, (npow, LANES), 0).astype(F32)
    lane_group = (lax.broadcasted_iota(jnp.int32, (h, r * LANES), 1) % LANES) // h

    def tile_rows(x):
        return jnp.broadcast_to(x[None], (r, h, LANES)).reshape(r16, LANES)

    def by_token(x):
        return x.reshape(r, h, LANES).astype(BF16)

    def rows_of(table, powers):
        return jnp.concatenate([jnp.broadcast_to(table[k:k + 1, :], (h, LANES)) for k in powers], axis=0)

    lam_r, lam_i = [], []
    for gi in range(gb):
        lr = lr_ref[gi:gi + 1, :]
        li = li_ref[gi:gi + 1, :]
        dt = jnp.exp(ls_ref[gi:gi + 1, :])
        mag = jnp.exp(kidx * (dt * lr))
        ang = kidx * (dt * li)
        pw_r, pw_i = mag * jnp.cos(ang), mag * jnp.sin(ang)
        lbr, lbi = pw_r[1:2, :], pw_i[1:2, :]
        den = lr * lr + li * li
        nr, ni = lbr - 1.0, lbi
        cr = (nr * lr + ni * li) / den
        ci = (ni * lr - nr * li) / den
        bre, bim = bre_ref[gi], bim_ref[gi]
        bbr = cr * bre - ci * bim
        bbi = cr * bim + ci * bre
        pr, pi = rows_of(pw_r, range(r + 1)), rows_of(pw_i, range(r + 1))
        prr, pir = rows_of(pw_r, range(r - 1, -1, -1)), rows_of(pw_i, range(r - 1, -1, -1))
        bbr_t, bbi_t = tile_rows(bbr), tile_rows(bbi)
        tin_ref[0, :, gi] = by_token(prr * bbr_t - pir * bbi_t)
        tin_ref[1, :, gi] = by_token(prr * bbi_t + pir * bbr_t)
        cr_t, ci_t = tile_rows(cre_ref[gi]), tile_rows(cim_ref[gi])
        pr1, pi1 = pr[h:], pi[h:]
        tout_ref[0, :, gi] = by_token(cr_t * pr1 - ci_t * pi1)
        tout_ref[1, :, gi] = by_token(-(cr_t * pi1 + ci_t * pr1))
        pr0, pi0 = pr[:r16], pi[:r16]
        wk = jnp.where(first, cr_t * pr0 - ci_t * pi0, -(cr_t * pi0 + ci_t * pr0))
        wk_t = jnp.broadcast_to(wk.reshape(r, 1, h, LANES), (r, gb, h, LANES)).reshape(r * gb * h, LANES)
        bb = jnp.where(first, bbr, bbi)
        mt = _dot_nt(bb.astype(BF16), wk_t.astype(BF16))
        toep_ref[gi * h:(gi + 1) * h, :] = jnp.where(lane_group == gi, mt, 0.0).astype(toep_ref.dtype)
        lam_r.append(pr[r * h:r * h + 1, :])
        lam_i.append(pi[r * h:r * h + 1, :])

    def pairs(rows):
        return jnp.concatenate([jnp.where(first, rows[2 * q], rows[2 * q + 1]) for q in range(gb // 2)], axis=1)

    lam_ref[0:1, :] = pairs(lam_r)
    lam_ref[1:2, :] = pairs(lam_i)


def s5_prep(lam_re, lam_im, log_step, b_re, b_im, c_re, c_im, r=S5_R, gb=S5_GB):
    g = lam_re.shape[0]
    h = S5_GROUP
    nlb = g // gb
    dbl = lambda x: jnp.concatenate([x, x], axis=-1)
    lr2, li2 = dbl(lam_re), dbl(lam_im)
    ls2 = jnp.broadcast_to(log_step[:, None], (g, LANES))
    bre2 = dbl(jnp.swapaxes(b_re, 1, 2))
    bim2 = dbl(jnp.swapaxes(b_im, 1, 2))
    cre2, cim2 = dbl(c_re), dbl(c_im)
    vec = pl.BlockSpec((gb, LANES), lambda i: (i, 0))
    mat = pl.BlockSpec((gb, h, LANES), lambda i: (i, 0, 0))
    tspec = pl.BlockSpec((None, 2, r, gb, h, LANES), lambda i: (i, 0, 0, 0, 0, 0))
    tshape = jax.ShapeDtypeStruct((nlb, 2, r, gb, h, LANES), BF16)
    kern = functools.partial(_s5_prep_kernel, gb=gb, r=r)
    tin, tout, toep, lam = pl.pallas_call(
        kern,
        grid=(nlb,),
        in_specs=[vec, vec, vec, mat, mat, mat, mat],
        out_specs=[tspec, tspec,
                   pl.BlockSpec((None, gb * h, r * LANES), lambda i: (i, 0, 0)),
                   pl.BlockSpec((None, 2, gb // 2 * LANES), lambda i: (i, 0, 0))],
        out_shape=[tshape, tshape,
                   jax.ShapeDtypeStruct((nlb, gb * h, r * LANES), BF16),
                   jax.ShapeDtypeStruct((nlb, 2, gb // 2 * LANES), F32)],
        compiler_params=_cparams(("parallel",)),
        name="s5_prep",
    )(lr2, li2, ls2, bre2, bim2, cre2, cim2)
    rows = r * gb * h
    return tin.reshape(nlb, 2, rows, LANES), tout.reshape(nlb, 2, rows, LANES), toep, lam


def _s5_pair_mask(r, gb):
    group = (np.arange(r * LANES) % LANES) // S5_GROUP
    lane = np.arange(gb // 2 * LANES)
    target = 2 * (lane // LANES) + (lane % LANES) // S5_STATE
    return jnp.asarray((group[:, None] == target[None, :]).astype(np.float32), dtype=BF16)


def _s5_kernel(a_ref, tin_ref, tout_ref, toep_ref, lam_ref, mask_ref, d_ref, y_ref,
               a32_scr, y32_scr, bdin_scr, bdout_scr, bdt_scr, s_scr, xp_scr, *, batch, jn, r, gb):
    bj = batch * jn
    npair = gb // 2
    half = npair * LANES
    a32_scr[...] = a_ref[...].astype(F32)
    a_tok = [a32_scr[pl.ds(i, bj, stride=r), :] for i in range(r)]

    for q in range(npair):
        mq = mask_ref[:, q * LANES:(q + 1) * LANES]
        for part in range(2):
            cols = slice((part * npair + q) * LANES, (part * npair + q + 1) * LANES)
            bdin_scr[:, cols] = tin_ref[part] * mq
            bdout_scr[:, cols] = tout_ref[part] * mq
    for i in range(r):
        if i > 0:
            bdt_scr[i * LANES:(i + 1) * LANES, :i * LANES] = jnp.zeros((LANES, i * LANES), bdt_scr.dtype)
        bdt_scr[i * LANES:(i + 1) * LANES, i * LANES:] = toep_ref[:, :(r - i) * LANES]

    a = jnp.concatenate([x.astype(BF16) for x in a_tok], axis=1)
    s_scr[...] = _dot(a, bdin_scr[...])
    lam_r = lam_ref[0:1, :]
    lam_i = lam_ref[1:2, :]

    def step(j, carry):
        out = []
        for b in range(batch):
            xr, xi = carry[2 * b], carry[2 * b + 1]
            row = pl.ds(b * jn + j, 1)
            xp_scr[row, :half] = xr
            xp_scr[row, half:] = xi
            s = s_scr[row, :]
            out.append(lam_r * xr - lam_i * xi + s[:, :half])
            out.append(lam_r * xi + lam_i * xr + s[:, half:])
        return tuple(out)

    zero = jnp.zeros((1, half), F32)
    lax.fori_loop(0, jn, step, (zero,) * (2 * batch))

    y_state = _dot_nt(xp_scr[...].astype(BF16), bdout_scr[...])
    d = d_ref[...]
    span = r // S5_TOEP_SPLIT
    for g in range(S5_TOEP_SPLIT):
        rows = (g + 1) * span * LANES
        cols = slice(g * span * LANES, rows)
        y = y_state[:, cols] + _dot(a[:, :rows], bdt_scr[:rows, cols])
        for i in range(g * span, (g + 1) * span):
            yi = y[:, (i - g * span) * LANES:(i - g * span + 1) * LANES] + a_tok[i] * d
            y32_scr[pl.ds(i, bj, stride=r), :] = 0.5 * yi * (1.0 + lax.erf(yi * (2.0 ** -0.5)))
    y_ref[...] = y32_scr[...].astype(y_ref.dtype)


def s5_mixer_gelu(a_norm, batch, seq, lam_re, lam_im, log_step, b_re, b_im, c_re, c_im, d_skip,
                  r=S5_R, gb=S5_GB):
    m, d = a_norm.shape
    assert gb * S5_GROUP == LANES
    nlb = d // LANES
    jn = seq // r
    bj = batch * jn
    half = gb // 2 * LANES
    tin, tout, toep, lam = s5_prep(lam_re, lam_im, log_step, b_re, b_im, c_re, c_im, r=r, gb=gb)
    lane_block = pl.BlockSpec((m, LANES), lambda lb: (0, lb))
    whole = lambda *shape: pl.BlockSpec((None,) + shape, lambda lb: (lb,) + (0,) * len(shape))
    kern = functools.partial(_s5_kernel, batch=batch, jn=jn, r=r, gb=gb)
    return pl.pallas_call(
        kern,
        grid=(nlb,),
        in_specs=[lane_block,
                  whole(2, r * LANES, LANES), whole(2, r * LANES, LANES), whole(LANES, r * LANES), whole(2, half),
                  pl.BlockSpec((r * LANES, half), lambda lb: (0, 0)),
                  pl.BlockSpec((1, LANES), lambda lb: (0, lb))],
        out_specs=lane_block,
        out_shape=jax.ShapeDtypeStruct((m, d), BF16),
        scratch_shapes=[pltpu.VMEM((m, LANES), F32), pltpu.VMEM((m, LANES), F32),
                        pltpu.VMEM((r * LANES, 2 * half), BF16), pltpu.VMEM((r * LANES, 2 * half), BF16),
                        pltpu.VMEM((r * LANES, r * LANES), BF16),
                        pltpu.VMEM((bj, 2 * half), F32), pltpu.VMEM((bj, 2 * half), F32)],
        compiler_params=_cparams(("parallel",)),
        name="s5_blocks",
    )(a_norm, tin, tout, toep, lam, _s5_pair_mask(r, gb), d_skip.reshape(1, d))


def kernel(x, norm_gains, w_in, w_gate_up, b_gate, gla_norm_gain, w_out, s5_lambda_re, s5_lambda_im,
           s5_log_step, s5_b_re, s5_b_im, s5_c_re, s5_c_im, s5_d, w_glu, w_ffn_in, w_ffn_out):
    batch, seq, d = x.shape
    m = batch * seq
    depth = norm_gains.shape[0]
    sb_w = SB_HEADS * SB_HEAD_DIM

    proj_w = -(-w_in.shape[2] // PROJ_TN) * PROJ_TN
    w_in_b = jnp.pad(w_in, ((0, 0), (0, 0), (0, proj_w - w_in.shape[2]))).astype(BF16)
    w_out_b, w_glu_b, w_ffn_out_b = w_out.astype(BF16), w_glu.astype(BF16), w_ffn_out.astype(BF16)

    h = x.astype(F32).reshape(m, d)
    a = rms_norm_bf16(h, norm_gains[0, 0])
    for layer in range(depth):
        gains = norm_gains[layer]
        i = layer // 2
        if layer % 2 == 0:
            proj = matmul(a, w_in_b, i, proj_w, BF16, tm=1024, tn=PROJ_TN)
            o_sb = sb_attention(proj, batch, seq)
            w_gate_pad = jnp.pad(w_gate_up[i], ((0, LANES - GLA_GATE_RANK), (0, 0)))
            o_gla = gla_attention(proj, w_gate_pad, b_gate[i], gla_norm_gain[i], batch, seq, 3 * sb_w)
            h, a = matmul_residual((o_sb, o_gla), w_out_b, i, h, gains[1], gains[2])
        else:
            y = s5_mixer_gelu(a, batch, seq, s5_lambda_re[i], s5_lambda_im[i], s5_log_step[i],
                              s5_b_re[i], s5_b_im[i], s5_c_re[i], s5_c_im[i], s5_d[i])
            h, a = matmul_residual(y, w_glu_b, i, h, gains[1], gains[2], glu=True, tm=1024, tk=d // 4,
                                   manual_h=True)
        f = ffn_in(a, w_ffn_in, layer)
        g_next = norm_gains[layer + 1, 0] if layer + 1 < depth else gains[3]
        h, a = matmul_residual(f, w_ffn_out_b, layer, h, gains[3], g_next, tm=1024,
                               tk=w_ffn_out.shape[1] // 4, manual_h=True)
    return h.reshape(batch, seq, d).astype(x.dtype)
```

```python
import functools
import math

import numpy as np
import jax
import jax.numpy as jnp
from jax import lax
from jax.experimental import pallas as pl
from jax.experimental.pallas import tpu as pltpu

F32 = jnp.float32
BF16 = jnp.bfloat16

NORM_EPS = 1e-6
LANES = 128
SB_HEADS = 8
SB_HEAD_DIM = 128
GLA_HEADS = 8
GLA_DK = 64
GLA_DV = 128
GLA_GATE_RANK = 16
GLA_GATE_TAU = 16.0
S5_GROUP = 16
S5_STATE = 64

SB_TQ = 1024
SB_TK = 256
GLA_CHUNK = 128
PROJ_TN = 1280
S5_R = 16
S5_GB = 8
S5_TOEP_SPLIT = 4
S5_HOP = 4

VMEM_LIMIT = 56 * 1024 * 1024

_HIGHEST = lax.Precision.HIGHEST
LOG2E = math.log2(math.e)


def _cparams(sem):
    return pltpu.CompilerParams(dimension_semantics=sem, vmem_limit_bytes=VMEM_LIMIT)


def _rms(x, g):
    return x * lax.rsqrt(jnp.mean(x * x, axis=-1, keepdims=True) + NORM_EPS) * g


def _dot(a, b):
    return jnp.dot(a, b, preferred_element_type=F32)


def _dot_nt(a, b):
    return lax.dot_general(a, b, (((1,), (1,)), ((), ())), preferred_element_type=F32)


def _dot_tn(a, b):
    return lax.dot_general(a, b, (((0,), (0,)), ((), ())), preferred_element_type=F32)


def _split_bf16(x):
    hi = x.astype(BF16)
    lo = (x - hi.astype(F32)).astype(BF16)
    return hi, lo


def _norm_kernel(h_ref, g_ref, o_ref):
    o_ref[...] = _rms(h_ref[...], g_ref[...]).astype(o_ref.dtype)


def rms_norm_bf16(h, g, tm=512):
    m, d = h.shape
    return pl.pallas_call(
        _norm_kernel,
        grid=(m // tm,),
        in_specs=[pl.BlockSpec((tm, d), lambda i: (i, 0)),
                  pl.BlockSpec((1, d), lambda i: (0, 0))],
        out_specs=pl.BlockSpec((tm, d), lambda i: (i, 0)),
        out_shape=jax.ShapeDtypeStruct((m, d), BF16),
        compiler_params=_cparams(("parallel",)),
        name="rms_norm",
    )(h, g.reshape(1, d))


def _mm_kernel(a_ref, w_ref, o_ref, wb_ref):
    @pl.when(pl.program_id(1) == 0)
    def _():
        wb_ref[...] = w_ref[...].astype(BF16)

    o_ref[...] = _dot(a_ref[...], wb_ref[...]).astype(o_ref.dtype)


def matmul(a, w, layer, ncols, out_dtype, tm, tn):
    m, k = a.shape
    return pl.pallas_call(
        _mm_kernel,
        grid=(ncols // tn, m // tm),
        in_specs=[pl.BlockSpec((tm, k), lambda j, i: (i, 0)),
                  pl.BlockSpec((None, k, tn), lambda j, i: (layer, 0, j))],
        out_specs=pl.BlockSpec((tm, tn), lambda j, i: (i, j)),
        out_shape=jax.ShapeDtypeStruct((m, ncols), out_dtype),
        scratch_shapes=[pltpu.VMEM((k, tn), BF16)],
        compiler_params=_cparams(("parallel", "arbitrary")),
        name="matmul",
    )(a, w)


def _ffn_in_kernel(a_ref, wg_ref, wu_ref, o_ref, wgb_ref, wub_ref):
    @pl.when(pl.program_id(1) == 0)
    def _():
        wgb_ref[...] = wg_ref[...].astype(BF16)
        wub_ref[...] = wu_ref[...].astype(BF16)

    a = a_ref[...]
    g = _dot(a, wgb_ref[...])
    u = _dot(a, wub_ref[...])
    o_ref[...] = (g * jax.nn.sigmoid(g) * u).astype(o_ref.dtype)


def ffn_in(a, w, layer, tm=2048, tn=512):
    m, k = a.shape
    nf = w.shape[2] // 2
    nj = nf // tn
    return pl.pallas_call(
        _ffn_in_kernel,
        grid=(nj, m // tm),
        in_specs=[pl.BlockSpec((tm, k), lambda j, i: (i, 0)),
                  pl.BlockSpec((None, k, tn), lambda j, i: (layer, 0, j)),
                  pl.BlockSpec((None, k, tn), lambda j, i: (layer, 0, j + nj))],
        out_specs=pl.BlockSpec((tm, tn), lambda j, i: (i, j)),
        out_shape=jax.ShapeDtypeStruct((m, nf), BF16),
        scratch_shapes=[pltpu.VMEM((k, tn), BF16), pltpu.VMEM((k, tn), BF16)],
        compiler_params=_cparams(("parallel", "arbitrary")),
        name="ffn_in",
    )(a, w, w)


def _mm_res_kernel(*refs, nk, glu, d, n_a, tm, manual_h):
    a_refs = refs[:n_a]
    w_ref, h_ref, gpost_ref, gnext_ref, hout_ref, anext_ref, acc_ref = refs[n_a:n_a + 7]
    i = pl.program_id(0)
    k = pl.program_id(1)

    def product():
        if n_a == 1:
            return _dot(a_refs[0][...], w_ref[...])
        out, lo = None, 0
        for ar in a_refs:
            part = _dot(ar[...], w_ref[lo:lo + ar.shape[1], :])
            out = part if out is None else out + part
            lo += ar.shape[1]
        return out

    def finish(h_tile):
        y = acc_ref[...]
        if glu:
            y = y[:, :d] * jax.nn.sigmoid(y[:, d:])
        hn = h_tile + _rms(y, gpost_ref[...])
        anext_ref[...] = _rms(hn, gnext_ref[...]).astype(anext_ref.dtype)
        return hn

    @pl.when(k == 0)
    def _():
        acc_ref[...] = product()

    @pl.when(k > 0)
    def _():
        acc_ref[...] += product()

    if not manual_h:
        @pl.when(k == nk - 1)
        def _():
            hout_ref[...] = finish(h_ref[...])
        return

    hbuf, sems = refs[n_a + 7:]
    nt = pl.num_programs(0)

    def tile_in(t):
        return pltpu.make_async_copy(h_ref.at[pl.ds(pl.multiple_of(t * tm, tm), tm), :], hbuf, sems.at[0])

    def tile_out(t):
        return pltpu.make_async_copy(hbuf, hout_ref.at[pl.ds(pl.multiple_of(t * tm, tm), tm), :], sems.at[1])

    @pl.when(k == 1)
    def _():
        @pl.when(i > 0)
        def _():
            tile_out(i - 1).wait()
        tile_in(i).start()

    @pl.when(k == nk - 1)
    def _():
        tile_in(i).wait()
        hbuf[...] = finish(hbuf[...])
        tile_out(i).start()

        @pl.when(i == nt - 1)
        def _():
            tile_out(i).wait()


def matmul_residual(a, w, layer, h, g_post, g_next, *, glu=False, tm=512, tk=512, manual_h=False):
    a_list = list(a) if isinstance(a, (tuple, list)) else [a]
    n_a = len(a_list)
    m = a_list[0].shape[0]
    kdim, n = w.shape[1:]
    d = h.shape[1]
    if n_a == 1:
        a_specs = [pl.BlockSpec((tm, tk), lambda i, k: (i, k))]
    else:
        tk = kdim
        assert sum(x.shape[1] for x in a_list) == kdim
        a_specs = [pl.BlockSpec((tm, x.shape[1]), lambda i, k: (i, 0)) for x in a_list]
    nk = kdim // tk
    assert nk >= 2 or not manual_h
    kern = functools.partial(_mm_res_kernel, nk=nk, glu=glu, d=d, n_a=n_a, tm=tm, manual_h=manual_h)
    h_spec = pl.BlockSpec(memory_space=pl.ANY) if manual_h else pl.BlockSpec((tm, d), lambda i, k: (i, 0))
    scratch = [pltpu.VMEM((tm, n), F32)]
    if manual_h:
        scratch += [pltpu.VMEM((tm, d), F32), pltpu.SemaphoreType.DMA((2,))]
    return pl.pallas_call(
        kern,
        grid=(m // tm, nk),
        in_specs=a_specs + [
                  pl.BlockSpec((None, tk, n), lambda i, k: (layer, k, 0)),
                  h_spec,
                  pl.BlockSpec((1, d), lambda i, k: (0, 0)),
                  pl.BlockSpec((1, d), lambda i, k: (0, 0))],
        out_specs=[h_spec, pl.BlockSpec((tm, d), lambda i, k: (i, 0))],
        out_shape=[jax.ShapeDtypeStruct((m, d), F32),
                   jax.ShapeDtypeStruct((m, d), BF16)],
        scratch_shapes=scratch,
        compiler_params=_cparams(("arbitrary" if manual_h else "parallel", "arbitrary")),
        name="matmul_residual",
    )(*a_list, w, h, g_post.reshape(1, d), g_next.reshape(1, d))


def _sb_kernel(q_ref, k_ref, v_ref, u_ref, o_ref, acc_ref, c_ref, *, tq, tk, scale):
    i = pl.program_id(2)
    q = (q_ref[...].astype(F32) * scale).astype(BF16)
    u = u_ref[...]
    acc_ref[...] = jnp.zeros_like(acc_ref)
    c_ref[...] = jnp.zeros_like(c_ref)

    def run(items):
        def rows(ref, it):
            return ref[pl.ds(pl.multiple_of(it[0] * tk, tk), tk), :]

        def logits(it):
            return _dot_nt(rows(k_ref, it), q[it[2], :])

        def keep_mask(it, shape):
            spos = it[1] + lax.broadcasted_iota(jnp.int32, shape, 0)
            tpos = it[2].start + lax.broadcasted_iota(jnp.int32, shape, 1)
            return spos < tpos

        def suffix(it, z):
            ls = jnp.minimum(z, 0.0) - jnp.log(1.0 + jnp.exp2(jnp.abs(z) * (-LOG2E)))
            lk = ls - z
            if it[1] is not None:
                lk = jnp.where(keep_mask(it, z.shape), lk, 0.0)
            later = _dot(u, lk.astype(BF16))
            return ls, later, later[0:1, :] + lk[0:1, :]

        def weighted(it, ls, later):
            w = jnp.exp(ls + later)
            if it[1] is not None:
                w = jnp.where(keep_mask(it, w.shape), w, 0.0)
            return _dot_tn(rows(v_ref, it), w.astype(BF16))

        zs = [logits(it) for it in items]
        sfx = [None] * len(items)
        pvs = [None] * len(items)
        sfx[0] = suffix(items[0], zs[0])
        for n in range(1, len(items)):
            sfx[n] = suffix(items[n], zs[n])
            pvs[n - 1] = weighted(items[n - 1], sfx[n - 1][0], sfx[n - 1][1])
        pvs[-1] = weighted(items[-1], sfx[-1][0], sfx[-1][1])
        for it, (_, _, total), pv in zip(items, sfx, pvs):
            acc_ref[:, it[2]] += jnp.exp(c_ref[:, it[2]]) * pv
            c_ref[:, it[2]] += total

    nstraddle = tq // tk
    items = []
    for s in range(nstraddle - 1, -1, -1):
        items.append((i * nstraddle + s, s * tk, slice(s * tk, (s + 1) * tk)))
        if s + 1 < nstraddle:
            items.append((i * nstraddle + s, None, slice((s + 1) * tk, tq)))
    run(items)

    def full_blocks(first, count):
        run([(first - n, None, slice(0, tq)) for n in range(count)])

    nfull = i * nstraddle
    group = 2 * nstraddle

    @pl.when(nfull % group != 0)
    def _():
        full_blocks(nfull - 1, nstraddle)

    def body(n, carry):
        full_blocks(nfull - nfull % group - 1 - group * n, group)
        return carry

    lax.fori_loop(0, nfull // group, body, 0)
    o_ref[...] = acc_ref[...].T.astype(o_ref.dtype)


def _sb_umat(tk):
    s = np.arange(tk)[:, None]
    j = np.arange(tk)[None, :]
    return jnp.asarray((j > s).astype(np.float32), dtype=BF16)


def sb_attention(proj, batch, seq, tq=SB_TQ, tk=SB_TK):
    d, nh = SB_HEAD_DIM, SB_HEADS
    assert (tq // tk) % 2 == 0
    nq = seq // tq
    kern = functools.partial(_sb_kernel, tq=tq, tk=tk, scale=d ** -0.5)
    return pl.pallas_call(
        kern,
        grid=(batch, nh, nq),
        in_specs=[pl.BlockSpec((tq, d), lambda b, h, i: (b * nq + i, h)),
                  pl.BlockSpec((seq, d), lambda b, h, i: (b, nh + h)),
                  pl.BlockSpec((seq, d), lambda b, h, i: (b, 2 * nh + h)),
                  pl.BlockSpec((tk, tk), lambda b, h, i: (0, 0))],
        out_specs=pl.BlockSpec((tq, d), lambda b, h, i: (b * nq + i, h)),
        out_shape=jax.ShapeDtypeStruct((batch * seq, nh * d), BF16),
        scratch_shapes=[pltpu.VMEM((d, tq), F32), pltpu.VMEM((1, tq), F32)],
        compiler_params=_cparams(("parallel", "parallel", "parallel")),
        name="sb_attention",
    )(proj, proj, proj, _sb_umat(tk))


def _gla_levels(c):
    return int(math.log2(c))


def _gla_masks(c):
    t = np.arange(c)[:, None]
    s = np.arange(c)[None, :]
    masks = []
    for lev in range(_gla_levels(c)):
        half = c >> (lev + 1)
        blk = 2 * half
        masks.append((t // blk == s // blk) & ((t % blk) >= half) & ((s % blk) < half))
    masks.append(t == s)
    m = np.concatenate(masks, axis=0).astype(np.float32)
    return jnp.asarray(np.concatenate([m, m], axis=1))


def _gla_boundary_rows(cum, half):
    c, width = cum.shape
    blk = 2 * half
    sub = 8
    if half >= sub:
        return jnp.concatenate(
            [jnp.broadcast_to(cum[b * blk + half - 1:b * blk + half, :], (blk, width)) for b in range(c // blk)],
            axis=0)
    x = cum.reshape(c // sub, sub, width)
    row = lax.broadcasted_iota(jnp.int32, (c // sub, sub, width), 1)
    out = None
    for b in range(sub // blk):
        piece = jnp.broadcast_to(x[:, b * blk + half - 1:b * blk + half, :], x.shape)
        out = piece if out is None else jnp.where(row >= b * blk, piece, out)
    return out.reshape(c, width)


def _gla_kernel(q_ref, k_ref, v_ref, r_ref, glr_ref, wg_ref, bg_ref, gain_ref, t_ref, m_ref, bm_ref,
                o_ref, st_ref, *, c, nlev):
    @pl.when(pl.program_id(1) == 0)
    def _():
        st_ref[...] = jnp.zeros_like(st_ref)

    dk, dv = GLA_DK, GLA_DV
    npair = GLA_HEADS // 2
    glr = glr_ref[...]
    wg = wg_ref[...]
    wg1 = wg.astype(BF16)
    rest = wg - wg1.astype(F32)
    wg2 = rest.astype(BF16)
    wg3 = (rest - wg2.astype(F32)).astype(BF16)
    logits = _dot(glr, wg1) + _dot(glr, wg2) + _dot(glr, wg3) + bg_ref[...]
    la = jax.nn.log_sigmoid(logits) / GLA_GATE_TAU
    hi, lo = _split_bf16(la)
    cum = _dot(t_ref[...], hi) + _dot(t_ref[...], lo)
    total = cum[c - 1:c, :]
    f_in = jnp.exp(cum)
    f_out = jnp.exp(total - cum)
    f_lev = [jnp.exp(-jnp.abs(cum - _gla_boundary_rows(cum, c >> (lev + 1)))) for lev in range(nlev)]

    q = q_ref[...].astype(F32) * (dk ** -0.5)
    k = k_ref[...].astype(F32)
    qs = [(q * f).astype(BF16) for f in f_lev] + [q.astype(BF16)]
    ks = [(k * f).astype(BF16) for f in f_lev] + [k.astype(BF16)]
    q_in = (q * f_in).astype(BF16)
    k_out = (k * f_out).astype(BF16)
    ones = jnp.ones((c, 2 * dv), BF16)
    head_a = lax.broadcasted_iota(jnp.int32, (c, 2 * dk), 1) < dk
    zero_k = jnp.zeros((c, 2 * dk), BF16)
    zero_v = jnp.zeros((c, dv), BF16)
    owned = [m_ref[lev * c:(lev + 1) * c, :] > 0.5 for lev in range(nlev + 1)]

    for p in range(npair):
        lanes = slice(p * 2 * dk, (p + 1) * 2 * dk)
        wide = slice(p * 2 * dv, (p + 1) * 2 * dv)
        scores = jnp.zeros((c, 2 * c), F32)
        for lev in range(nlev + 1):
            kp = ks[lev][:, lanes]
            kstack = jnp.concatenate([jnp.where(head_a, kp, zero_k), jnp.where(head_a, zero_k, kp)], axis=0)
            scores = jnp.where(owned[lev], _dot_nt(qs[lev][:, lanes], kstack), scores)
        v = v_ref[:, wide]
        v_bd = jnp.concatenate([jnp.concatenate([v[:, :dv], zero_v], axis=1),
                                jnp.concatenate([zero_v, v[:, dv:]], axis=1)], axis=0)
        st = st_ref[p]
        o = _dot(scores.astype(BF16), v_bd) + _dot(q_in[:, lanes], st.astype(BF16))
        tot_col = _dot_tn(hi[:, lanes], ones) + _dot_tn(lo[:, lanes], ones)
        st_ref[p] = jnp.exp(tot_col) * st + bm_ref[...] * _dot_tn(k_out[:, lanes], v)
        for hd in range(2):
            oh = o[:, hd * dv:(hd + 1) * dv]
            oh = oh * lax.rsqrt(jnp.mean(oh * oh, axis=-1, keepdims=True) + NORM_EPS)
            cols = slice(p * 2 * dv + hd * dv, p * 2 * dv + (hd + 1) * dv)
            rr = r_ref[:, cols].astype(F32)
            o_ref[:, cols] = (oh * gain_ref[:, cols] * (rr * jax.nn.sigmoid(rr))).astype(o_ref.dtype)


def gla_attention(proj, w_gate_pad, b_gate, gain, batch, seq, col0, c=GLA_CHUNK):
    dk, dv, nh = GLA_DK, GLA_DV, GLA_HEADS
    nc = seq // c
    nlev = _gla_levels(c)
    kw, vw = nh * dk, nh * dv
    qb = col0 // kw
    vb = (col0 + 2 * kw) // vw
    lrb = (col0 + 2 * kw + 2 * vw) // LANES
    tincl = jnp.asarray(np.tril(np.ones((c, c), np.float32)), dtype=BF16)
    pair_blocks = jnp.asarray(np.kron(np.eye(2, dtype=np.float32), np.ones((dk, dv), np.float32)))
    kern = functools.partial(_gla_kernel, c=c, nlev=nlev)
    row = lambda b, n: b * nc + n
    const = lambda shape: pl.BlockSpec(shape, lambda b, n: (0, 0))
    return pl.pallas_call(
        kern,
        grid=(batch, nc),
        in_specs=[pl.BlockSpec((c, kw), lambda b, n: (row(b, n), qb)),
                  pl.BlockSpec((c, kw), lambda b, n: (row(b, n), qb + 1)),
                  pl.BlockSpec((c, vw), lambda b, n: (row(b, n), vb)),
                  pl.BlockSpec((c, vw), lambda b, n: (row(b, n), vb + 1)),
                  pl.BlockSpec((c, LANES), lambda b, n: (row(b, n), lrb)),
                  const((LANES, kw)), const((1, kw)), const((1, vw)),
                  const((c, c)), const(((nlev + 1) * c, 2 * c)), const((2 * dk, 2 * dv))],
        out_specs=pl.BlockSpec((c, vw), lambda b, n: (row(b, n), 0)),
        out_shape=jax.ShapeDtypeStruct((batch * seq, vw), BF16),
        scratch_shapes=[pltpu.VMEM((nh // 2, 2 * dk, 2 * dv), F32)],
        compiler_params=_cparams(("parallel", "arbitrary")),
        name="gla",
    )(proj, proj, proj, proj, proj, w_gate_pad, b_gate.reshape(1, -1), gain.reshape(1, -1),
      tincl, _gla_masks(c), pair_blocks)


def _s5_prep_kernel(lr_ref, li_ref, ls_ref, bre_ref, bim_ref, cre_ref, cim_ref,
                    tin_ref, tout_ref, toep_ref, lam_ref, *, gb, r):
    h = S5_GROUP
    r16 = r * h
    first = lax.broadcasted_iota(jnp.int32, (1, LANES), 1) < S5_STATE
    npow = -(-(r + 1) // 8) * 8
    kidx = lax.broadcasted_iota(jnp.int32, (npow, LANES), 0).astype(F32)
    lane_group = (lax.broadcasted_iota(jnp.int32, (h, r * LANES), 1) % LANES) // h

    def tile_rows(x):
        return jnp.broadcast_to(x[None], (r, h, LANES)).reshape(r16, LANES)

    def by_token(x):
        return x.reshape(r, h, LANES).astype(BF16)

    def rows_of(table, powers):
        return jnp.concatenate([jnp.broadcast_to(table[k:k + 1, :], (h, LANES)) for k in powers], axis=0)

    lam_r, lam_i = [], []
    for gi in range(gb):
        lr = lr_ref[gi:gi + 1, :]
        li = li_ref[gi:gi + 1, :]
        dt = jnp.exp(ls_ref[gi:gi + 1, :])
        mag = jnp.exp(kidx * (dt * lr))
        ang = kidx * (dt * li)
        pw_r, pw_i = mag * jnp.cos(ang), mag * jnp.sin(ang)
        lbr, lbi = pw_r[1:2, :], pw_i[1:2, :]
        den = lr * lr + li * li
        nr, ni = lbr - 1.0, lbi
        cr = (nr * lr + ni * li) / den
        ci = (ni * lr - nr * li) / den
        bre, bim = bre_ref[gi], bim_ref[gi]
        bbr = cr * bre - ci * bim
        bbi = cr * bim + ci * bre
        pr, pi = rows_of(pw_r, range(r + 1)), rows_of(pw_i, range(r + 1))
        prr, pir = rows_of(pw_r, range(r - 1, -1, -1)), rows_of(pw_i, range(r - 1, -1, -1))
        bbr_t, bbi_t = tile_rows(bbr), tile_rows(bbi)
        tin_ref[0, :, gi] = by_token(prr * bbr_t - pir * bbi_t)
        tin_ref[1, :, gi] = by_token(prr * bbi_t + pir * bbr_t)
        cr_t, ci_t = tile_rows(cre_ref[gi]), tile_rows(cim_ref[gi])
        pr1, pi1 = pr[h:], pi[h:]
        tout_ref[0, :, gi] = by_token(cr_t * pr1 - ci_t * pi1)
        tout_ref[1, :, gi] = by_token(-(cr_t * pi1 + ci_t * pr1))
        pr0, pi0 = pr[:r16], pi[:r16]
        wk = jnp.where(first, cr_t * pr0 - ci_t * pi0, -(cr_t * pi0 + ci_t * pr0))
        wk_t = jnp.broadcast_to(wk.reshape(r, 1, h, LANES), (r, gb, h, LANES)).reshape(r * gb * h, LANES)
        bb = jnp.where(first, bbr, bbi)
        mt = _dot_nt(bb.astype(BF16), wk_t.astype(BF16))
        toep_ref[gi * h:(gi + 1) * h, :] = jnp.where(lane_group == gi, mt, 0.0).astype(toep_ref.dtype)
        lam_r.append(pr[r * h:r * h + 1, :])
        lam_i.append(pi[r * h:r * h + 1, :])

    def pairs(rows):
        return jnp.concatenate([jnp.where(first, rows[2 * q], rows[2 * q + 1]) for q in range(gb // 2)], axis=1)

    lam_ref[0:1, :] = pairs(lam_r)
    lam_ref[1:2, :] = pairs(lam_i)


def s5_prep(lam_re, lam_im, log_step, b_re, b_im, c_re, c_im, r=S5_R, gb=S5_GB):
    g = lam_re.shape[0]
    h = S5_GROUP
    nlb = g // gb
    dbl = lambda x: jnp.concatenate([x, x], axis=-1)
    lr2, li2 = dbl(lam_re), dbl(lam_im)
    ls2 = jnp.broadcast_to(log_step[:, None], (g, LANES))
    bre2 = dbl(jnp.swapaxes(b_re, 1, 2))
    bim2 = dbl(jnp.swapaxes(b_im, 1, 2))
    cre2, cim2 = dbl(c_re), dbl(c_im)
    vec = pl.BlockSpec((gb, LANES), lambda i: (i, 0))
    mat = pl.BlockSpec((gb, h, LANES), lambda i: (i, 0, 0))
    tspec = pl.BlockSpec((None, 2, r, gb, h, LANES), lambda i: (i, 0, 0, 0, 0, 0))
    tshape = jax.ShapeDtypeStruct((nlb, 2, r, gb, h, LANES), BF16)
    kern = functools.partial(_s5_prep_kernel, gb=gb, r=r)
    tin, tout, toep, lam = pl.pallas_call(
        kern,
        grid=(nlb,),
        in_specs=[vec, vec, vec, mat, mat, mat, mat],
        out_specs=[tspec, tspec,
                   pl.BlockSpec((None, gb * h, r * LANES), lambda i: (i, 0, 0)),
                   pl.BlockSpec((None, 2, gb // 2 * LANES), lambda i: (i, 0, 0))],
        out_shape=[tshape, tshape,
                   jax.ShapeDtypeStruct((nlb, gb * h, r * LANES), BF16),
                   jax.ShapeDtypeStruct((nlb, 2, gb // 2 * LANES), F32)],
        compiler_params=_cparams(("parallel",)),
        name="s5_prep",
    )(lr2, li2, ls2, bre2, bim2, cre2, cim2)
    rows = r * gb * h
    return tin.reshape(nlb, 2, rows, LANES), tout.reshape(nlb, 2, rows, LANES), toep, lam


def _s5_pair_mask(r, gb):
    group = (np.arange(r * LANES) % LANES) // S5_GROUP
    lane = np.arange(gb // 2 * LANES)
    target = 2 * (lane // LANES) + (lane % LANES) // S5_STATE
    return jnp.asarray((group[:, None] == target[None, :]).astype(np.float32), dtype=BF16)


def _s5_kernel(a_ref, tin_ref, tout_ref, toep_ref, lam_ref, mask_ref, d_ref, y_ref,
               tok_scr, in_scr, out_scr, bdin_scr, bdout_scr, bdt_scr, s_scr, xp_scr, *, batch, jn, r, gb):
    bj = batch * jn
    npair = gb // 2
    half = npair * LANES
    hop = S5_HOP
    assert r == hop * hop
    tok_scr[...] = a_ref[...].astype(F32)
    for c in range(hop):
        in_scr[c] = tok_scr[pl.ds(c, bj * hop, stride=hop), :]

    def a_tok(i):
        return in_scr[i % hop, pl.ds(i // hop, bj, stride=hop), :]

    for q in range(npair):
        mq = mask_ref[:, q * LANES:(q + 1) * LANES]
        for part in range(2):
            cols = slice((part * npair + q) * LANES, (part * npair + q + 1) * LANES)
            bdin_scr[:, cols] = tin_ref[part] * mq
            bdout_scr[:, cols] = tout_ref[part] * mq
    for i in range(r):
        if i > 0:
            bdt_scr[i * LANES:(i + 1) * LANES, :i * LANES] = jnp.zeros((LANES, i * LANES), bdt_scr.dtype)
        bdt_scr[i * LANES:(i + 1) * LANES, i * LANES:] = toep_ref[:, :(r - i) * LANES]

    a = jnp.concatenate([a_tok(i).astype(BF16) for i in range(r)], axis=1)
    s_scr[...] = _dot(a, bdin_scr[...])
    lam_r = lam_ref[0:1, :]
    lam_i = lam_ref[1:2, :]

    def step(j, carry):
        out = []
        for b in range(batch):
            xr, xi = carry[2 * b], carry[2 * b + 1]
            row = pl.ds(b * jn + j, 1)
            xp_scr[row, :half] = xr
            xp_scr[row, half:] = xi
            s = s_scr[row, :]
            out.append(lam_r * xr - lam_i * xi + s[:, :half])
            out.append(lam_r * xi + lam_i * xr + s[:, half:])
        return tuple(out)

    zero = jnp.zeros((1, half), F32)
    lax.fori_loop(0, jn, step, (zero,) * (2 * batch))

    y_state = _dot_nt(xp_scr[...].astype(BF16), bdout_scr[...])
    d = d_ref[...]
    span = r // S5_TOEP_SPLIT
    for g in range(S5_TOEP_SPLIT):
        rows = (g + 1) * span * LANES
        cols = slice(g * span * LANES, rows)
        y = y_state[:, cols] + _dot(a[:, :rows], bdt_scr[:rows, cols])
        for i in range(g * span, (g + 1) * span):
            yi = y[:, (i - g * span) * LANES:(i - g * span + 1) * LANES] + a_tok(i) * d
            out_scr[i % hop, pl.ds(i // hop, bj, stride=hop), :] = 0.5 * yi * (1.0 + lax.erf(yi * (2.0 ** -0.5)))
    for c in range(hop):
        tok_scr[pl.ds(c, bj * hop, stride=hop), :] = out_scr[c]
    y_ref[...] = tok_scr[...].astype(y_ref.dtype)


def s5_mixer_gelu(a_norm, batch, seq, lam_re, lam_im, log_step, b_re, b_im, c_re, c_im, d_skip,
                  r=S5_R, gb=S5_GB):
    m, d = a_norm.shape
    assert gb * S5_GROUP == LANES
    nlb = d // LANES
    jn = seq // r
    bj = batch * jn
    half = gb // 2 * LANES
    tin, tout, toep, lam = s5_prep(lam_re, lam_im, log_step, b_re, b_im, c_re, c_im, r=r, gb=gb)
    lane_block = pl.BlockSpec((m, LANES), lambda lb: (0, lb))
    whole = lambda *shape: pl.BlockSpec((None,) + shape, lambda lb: (lb,) + (0,) * len(shape))
    kern = functools.partial(_s5_kernel, batch=batch, jn=jn, r=r, gb=gb)
    return pl.pallas_call(
        kern,
        grid=(nlb,),
        in_specs=[lane_block,
                  whole(2, r * LANES, LANES), whole(2, r * LANES, LANES), whole(LANES, r * LANES), whole(2, half),
                  pl.BlockSpec((r * LANES, half), lambda lb: (0, 0)),
                  pl.BlockSpec((1, LANES), lambda lb: (0, lb))],
        out_specs=lane_block,
        out_shape=jax.ShapeDtypeStruct((m, d), BF16),
        scratch_shapes=[pltpu.VMEM((m, LANES), F32),
                        pltpu.VMEM((S5_HOP, m // S5_HOP, LANES), F32), pltpu.VMEM((S5_HOP, m // S5_HOP, LANES), F32),
                        pltpu.VMEM((r * LANES, 2 * half), BF16), pltpu.VMEM((r * LANES, 2 * half), BF16),
                        pltpu.VMEM((r * LANES, r * LANES), BF16),
                        pltpu.VMEM((bj, 2 * half), F32), pltpu.VMEM((bj, 2 * half), F32)],
        compiler_params=_cparams(("parallel",)),
        name="s5_blocks",
    )(a_norm, tin, tout, toep, lam, _s5_pair_mask(r, gb), d_skip.reshape(1, d))


def kernel(x, norm_gains, w_in, w_gate_up, b_gate, gla_norm_gain, w_out, s5_lambda_re, s5_lambda_im,
           s5_log_step, s5_b_re, s5_b_im, s5_c_re, s5_c_im, s5_d, w_glu, w_ffn_in, w_ffn_out):
    batch, seq, d = x.shape
    m = batch * seq
    depth = norm_gains.shape[0]
    sb_w = SB_HEADS * SB_HEAD_DIM

    proj_w = -(-w_in.shape[2] // PROJ_TN) * PROJ_TN
    w_in_b = jnp.concatenate(
        [w_in.astype(BF16), jnp.zeros(w_in.shape[:2] + (proj_w - w_in.shape[2],), BF16)], axis=2)
    w_out_b, w_glu_b, w_ffn_out_b = w_out.astype(BF16), w_glu.astype(BF16), w_ffn_out.astype(BF16)

    h = x.astype(F32).reshape(m, d)
    a = rms_norm_bf16(h, norm_gains[0, 0])
    for layer in range(depth):
        gains = norm_gains[layer]
        i = layer // 2
        if layer % 2 == 0:
            proj = matmul(a, w_in_b, i, proj_w, BF16, tm=1024, tn=PROJ_TN)
            o_sb = sb_attention(proj, batch, seq)
            w_gate_pad = jnp.pad(w_gate_up[i], ((0, LANES - GLA_GATE_RANK), (0, 0)))
            o_gla = gla_attention(proj, w_gate_pad, b_gate[i], gla_norm_gain[i], batch, seq, 3 * sb_w)
            h, a = matmul_residual((o_sb, o_gla), w_out_b, i, h, gains[1], gains[2])
        else:
            y = s5_mixer_gelu(a, batch, seq, s5_lambda_re[i], s5_lambda_im[i], s5_log_step[i],
                              s5_b_re[i], s5_b_im[i], s5_c_re[i], s5_c_im[i], s5_d[i])
            h, a = matmul_residual(y, w_glu_b, i, h, gains[1], gains[2], glu=True, tm=1024, tk=d // 4,
                                   manual_h=True)
        f = ffn_in(a, w_ffn_in, layer)
        g_next = norm_gains[layer + 1, 0] if layer + 1 < depth else gains[3]
        h, a = matmul_residual(f, w_ffn_out_b, layer, h, gains[3], g_next, tm=1024,
                               tk=w_ffn_out.shape[1] // 4, manual_h=True)
    return h.reshape(batch, seq, d).astype(x.dtype)
```

```python
import functools
import math

import numpy as np
import jax
import jax.numpy as jnp
from jax import lax
from jax.experimental import pallas as pl
from jax.experimental.pallas import tpu as pltpu

F32 = jnp.float32
BF16 = jnp.bfloat16

NORM_EPS = 1e-6
LANES = 128
SB_HEADS = 8
SB_HEAD_DIM = 128
GLA_HEADS = 8
GLA_DK = 64
GLA_DV = 128
GLA_GATE_RANK = 16
GLA_GATE_TAU = 16.0
S5_GROUP = 16
S5_STATE = 64

SB_TQ = 1024
SB_TK = 256
GLA_CHUNK = 128
PROJ_TN = 1280
S5_R = 16
S5_GB = 8
S5_TOEP_SPLIT = 4
S5_HOP = 4

VMEM_LIMIT = 56 * 1024 * 1024

LOG2E = math.log2(math.e)


def _cparams(sem):
    return pltpu.CompilerParams(dimension_semantics=sem, vmem_limit_bytes=VMEM_LIMIT)


def _rms(x, g):
    return x * lax.rsqrt(jnp.mean(x * x, axis=-1, keepdims=True) + NORM_EPS) * g


def _dot(a, b):
    return jnp.dot(a, b, preferred_element_type=F32)


def _dot_nt(a, b):
    return lax.dot_general(a, b, (((1,), (1,)), ((), ())), preferred_element_type=F32)


def _dot_tn(a, b):
    return lax.dot_general(a, b, (((0,), (0,)), ((), ())), preferred_element_type=F32)


def _split_bf16(x):
    hi = x.astype(BF16)
    lo = (x - hi.astype(F32)).astype(BF16)
    return hi, lo


def _norm_kernel(h_ref, g_ref, o_ref):
    o_ref[...] = _rms(h_ref[...], g_ref[...]).astype(o_ref.dtype)


def rms_norm_bf16(h, g, tm=512):
    m, d = h.shape
    return pl.pallas_call(
        _norm_kernel,
        grid=(m // tm,),
        in_specs=[pl.BlockSpec((tm, d), lambda i: (i, 0)),
                  pl.BlockSpec((1, d), lambda i: (0, 0))],
        out_specs=pl.BlockSpec((tm, d), lambda i: (i, 0)),
        out_shape=jax.ShapeDtypeStruct((m, d), BF16),
        compiler_params=_cparams(("parallel",)),
        name="rms_norm",
    )(h, g.reshape(1, d))


def _mm_kernel(a_ref, w_ref, o_ref, wb_ref):
    @pl.when(pl.program_id(1) == 0)
    def _():
        wb_ref[...] = w_ref[...].astype(BF16)

    o_ref[...] = _dot(a_ref[...], wb_ref[...]).astype(o_ref.dtype)


def matmul(a, w, layer, ncols, out_dtype, tm, tn):
    m, k = a.shape
    return pl.pallas_call(
        _mm_kernel,
        grid=(ncols // tn, m // tm),
        in_specs=[pl.BlockSpec((tm, k), lambda j, i: (i, 0)),
                  pl.BlockSpec((None, k, tn), lambda j, i: (layer, 0, j))],
        out_specs=pl.BlockSpec((tm, tn), lambda j, i: (i, j)),
        out_shape=jax.ShapeDtypeStruct((m, ncols), out_dtype),
        scratch_shapes=[pltpu.VMEM((k, tn), BF16)],
        compiler_params=_cparams(("parallel", "arbitrary")),
        name="matmul",
    )(a, w)


def _ffn_in_kernel(a_ref, wg_ref, wu_ref, o_ref, wgb_ref, wub_ref):
    @pl.when(pl.program_id(1) == 0)
    def _():
        wgb_ref[...] = wg_ref[...].astype(BF16)
        wub_ref[...] = wu_ref[...].astype(BF16)

    a = a_ref[...]
    g = _dot(a, wgb_ref[...])
    u = _dot(a, wub_ref[...])
    o_ref[...] = (g * jax.nn.sigmoid(g) * u).astype(o_ref.dtype)


def ffn_in(a, w, layer, tm=2048, tn=512):
    m, k = a.shape
    nf = w.shape[2] // 2
    nj = nf // tn
    return pl.pallas_call(
        _ffn_in_kernel,
        grid=(nj, m // tm),
        in_specs=[pl.BlockSpec((tm, k), lambda j, i: (i, 0)),
                  pl.BlockSpec((None, k, tn), lambda j, i: (layer, 0, j)),
                  pl.BlockSpec((None, k, tn), lambda j, i: (layer, 0, j + nj))],
        out_specs=pl.BlockSpec((tm, tn), lambda j, i: (i, j)),
        out_shape=jax.ShapeDtypeStruct((m, nf), BF16),
        scratch_shapes=[pltpu.VMEM((k, tn), BF16), pltpu.VMEM((k, tn), BF16)],
        compiler_params=_cparams(("parallel", "arbitrary")),
        name="ffn_in",
    )(a, w, w)


def _mm_res_kernel(*refs, nk, glu, d, n_a, tm, manual_h):
    a_refs = refs[:n_a]
    w_ref, h_ref, gpost_ref, gnext_ref, hout_ref, anext_ref, acc_ref = refs[n_a:n_a + 7]
    i = pl.program_id(0)
    k = pl.program_id(1)

    def product():
        if n_a == 1:
            return _dot(a_refs[0][...], w_ref[...])
        out, lo = None, 0
        for ar in a_refs:
            part = _dot(ar[...], w_ref[lo:lo + ar.shape[1], :])
            out = part if out is None else out + part
            lo += ar.shape[1]
        return out

    def finish(h_tile):
        y = acc_ref[...]
        if glu:
            y = y[:, :d] * jax.nn.sigmoid(y[:, d:])
        hn = h_tile + _rms(y, gpost_ref[...])
        anext_ref[...] = _rms(hn, gnext_ref[...]).astype(anext_ref.dtype)
        return hn

    @pl.when(k == 0)
    def _():
        acc_ref[...] = product()

    @pl.when(k > 0)
    def _():
        acc_ref[...] += product()

    if not manual_h:
        @pl.when(k == nk - 1)
        def _():
            hout_ref[...] = finish(h_ref[...])
        return

    hbuf, sems = refs[n_a + 7:]
    nt = pl.num_programs(0)

    def tile_in(t):
        return pltpu.make_async_copy(h_ref.at[pl.ds(pl.multiple_of(t * tm, tm), tm), :], hbuf, sems.at[0])

    def tile_out(t):
        return pltpu.make_async_copy(hbuf, hout_ref.at[pl.ds(pl.multiple_of(t * tm, tm), tm), :], sems.at[1])

    @pl.when(k == 1)
    def _():
        @pl.when(i > 0)
        def _():
            tile_out(i - 1).wait()
        tile_in(i).start()

    @pl.when(k == nk - 1)
    def _():
        tile_in(i).wait()
        hbuf[...] = finish(hbuf[...])
        tile_out(i).start()

        @pl.when(i == nt - 1)
        def _():
            tile_out(i).wait()


def matmul_residual(a, w, layer, h, g_post, g_next, *, glu=False, tm=512, tk=512, manual_h=False):
    a_list = list(a) if isinstance(a, (tuple, list)) else [a]
    n_a = len(a_list)
    m = a_list[0].shape[0]
    kdim, n = w.shape[1:]
    d = h.shape[1]
    if n_a == 1:
        a_specs = [pl.BlockSpec((tm, tk), lambda i, k: (i, k))]
    else:
        tk = kdim
        assert sum(x.shape[1] for x in a_list) == kdim
        a_specs = [pl.BlockSpec((tm, x.shape[1]), lambda i, k: (i, 0)) for x in a_list]
    nk = kdim // tk
    assert nk >= 2 or not manual_h
    kern = functools.partial(_mm_res_kernel, nk=nk, glu=glu, d=d, n_a=n_a, tm=tm, manual_h=manual_h)
    h_spec = pl.BlockSpec(memory_space=pl.ANY) if manual_h else pl.BlockSpec((tm, d), lambda i, k: (i, 0))
    scratch = [pltpu.VMEM((tm, n), F32)]
    if manual_h:
        scratch += [pltpu.VMEM((tm, d), F32), pltpu.SemaphoreType.DMA((2,))]
    return pl.pallas_call(
        kern,
        grid=(m // tm, nk),
        in_specs=a_specs + [
                  pl.BlockSpec((None, tk, n), lambda i, k: (layer, k, 0)),
                  h_spec,
                  pl.BlockSpec((1, d), lambda i, k: (0, 0)),
                  pl.BlockSpec((1, d), lambda i, k: (0, 0))],
        out_specs=[h_spec, pl.BlockSpec((tm, d), lambda i, k: (i, 0))],
        out_shape=[jax.ShapeDtypeStruct((m, d), F32),
                   jax.ShapeDtypeStruct((m, d), BF16)],
        scratch_shapes=scratch,
        compiler_params=_cparams(("arbitrary" if manual_h else "parallel", "arbitrary")),
        name="matmul_residual",
    )(*a_list, w, h, g_post.reshape(1, d), g_next.reshape(1, d))


def _sb_kernel(q_ref, k_ref, v_ref, u_ref, o_ref, acc_ref, c_ref, *, tq, tk, scale):
    i = pl.program_id(2)
    q = (q_ref[...].astype(F32) * scale).astype(BF16)
    u = u_ref[...]
    acc_ref[...] = jnp.zeros_like(acc_ref)
    c_ref[...] = jnp.zeros_like(c_ref)

    diag_keep = (lax.broadcasted_iota(jnp.int32, (tk, tk), 0) <
                 lax.broadcasted_iota(jnp.int32, (tk, tk), 1))

    def run(items):
        def rows(ref, it):
            return ref[pl.ds(pl.multiple_of(it[0] * tk, tk), tk), :]

        def logits(it):
            return _dot_nt(rows(k_ref, it), q[it[2], :])

        def suffix(it, z):
            ls = jnp.minimum(z, 0.0) - jnp.log(1.0 + jnp.exp2(jnp.abs(z) * (-LOG2E)))
            lk = ls - z
            if it[1]:
                lk = jnp.where(diag_keep, lk, 0.0)
            later = _dot(u, lk.astype(BF16))
            return ls, later, later[0:1, :] + lk[0:1, :]

        def weighted(it, ls, later):
            w = jnp.exp(ls + later)
            if it[1]:
                w = jnp.where(diag_keep, w, 0.0)
            return _dot_tn(rows(v_ref, it), w.astype(BF16))

        zs = [logits(it) for it in items]
        sfx = [None] * len(items)
        pvs = [None] * len(items)
        sfx[0] = suffix(items[0], zs[0])
        for n in range(1, len(items)):
            sfx[n] = suffix(items[n], zs[n])
            pvs[n - 1] = weighted(items[n - 1], sfx[n - 1][0], sfx[n - 1][1])
        pvs[-1] = weighted(items[-1], sfx[-1][0], sfx[-1][1])
        for it, (_, _, total), pv in zip(items, sfx, pvs):
            acc_ref[:, it[2]] += jnp.exp(c_ref[:, it[2]]) * pv
            c_ref[:, it[2]] += total

    nstraddle = tq // tk
    items = []
    for s in range(nstraddle - 1, -1, -1):
        items.append((i * nstraddle + s, True, slice(s * tk, (s + 1) * tk)))
        if s + 1 < nstraddle:
            items.append((i * nstraddle + s, False, slice((s + 1) * tk, tq)))
    run(items)

    def full_blocks(first, count):
        run([(first - n, False, slice(0, tq)) for n in range(count)])

    nfull = i * nstraddle
    group = 2 * nstraddle

    @pl.when(nfull % group != 0)
    def _():
        full_blocks(nfull - 1, nstraddle)

    def body(n, carry):
        full_blocks(nfull - nfull % group - 1 - group * n, group)
        return carry

    lax.fori_loop(0, nfull // group, body, 0)
    o_ref[...] = acc_ref[...].T.astype(o_ref.dtype)


def _sb_umat(tk):
    s = np.arange(tk)[:, None]
    j = np.arange(tk)[None, :]
    return jnp.asarray((j > s).astype(np.float32), dtype=BF16)


def sb_attention(proj, batch, seq, tq=SB_TQ, tk=SB_TK):
    d, nh = SB_HEAD_DIM, SB_HEADS
    assert (tq // tk) % 2 == 0
    nq = seq // tq
    kern = functools.partial(_sb_kernel, tq=tq, tk=tk, scale=d ** -0.5)
    return pl.pallas_call(
        kern,
        grid=(batch, nh, nq),
        in_specs=[pl.BlockSpec((tq, d), lambda b, h, i: (b * nq + i, h)),
                  pl.BlockSpec((seq, d), lambda b, h, i: (b, nh + h)),
                  pl.BlockSpec((seq, d), lambda b, h, i: (b, 2 * nh + h)),
                  pl.BlockSpec((tk, tk), lambda b, h, i: (0, 0))],
        out_specs=pl.BlockSpec((tq, d), lambda b, h, i: (b * nq + i, h)),
        out_shape=jax.ShapeDtypeStruct((batch * seq, nh * d), BF16),
        scratch_shapes=[pltpu.VMEM((d, tq), F32), pltpu.VMEM((1, tq), F32)],
        compiler_params=_cparams(("parallel", "parallel", "parallel")),
        name="sb_attention",
    )(proj, proj, proj, _sb_umat(tk))


def _gla_levels(c):
    return int(math.log2(c))


def _gla_masks(c):
    t = np.arange(c)[:, None]
    s = np.arange(c)[None, :]
    masks = []
    for lev in range(_gla_levels(c)):
        half = c >> (lev + 1)
        blk = 2 * half
        masks.append((t // blk == s // blk) & ((t % blk) >= half) & ((s % blk) < half))
    masks.append(t == s)
    m = np.concatenate(masks, axis=0).astype(np.float32)
    return jnp.asarray(np.concatenate([m, m], axis=1))


def _gla_boundary_rows(cum, half):
    c, width = cum.shape
    blk = 2 * half
    sub = 8
    if half >= sub:
        return jnp.concatenate(
            [jnp.broadcast_to(cum[b * blk + half - 1:b * blk + half, :], (blk, width)) for b in range(c // blk)],
            axis=0)
    x = cum.reshape(c // sub, sub, width)
    row = lax.broadcasted_iota(jnp.int32, (c // sub, sub, width), 1)
    out = None
    for b in range(sub // blk):
        piece = jnp.broadcast_to(x[:, b * blk + half - 1:b * blk + half, :], x.shape)
        out = piece if out is None else jnp.where(row >= b * blk, piece, out)
    return out.reshape(c, width)


def _gla_kernel(q_ref, k_ref, v_ref, r_ref, glr_ref, wg_ref, bg_ref, gain_ref, t_ref, m_ref, bm_ref,
                o_ref, st_ref, *, c, nlev):
    @pl.when(pl.program_id(1) == 0)
    def _():
        st_ref[...] = jnp.zeros_like(st_ref)

    dk, dv = GLA_DK, GLA_DV
    npair = GLA_HEADS // 2
    glr = glr_ref[...]
    wg = wg_ref[...]
    wg1 = wg.astype(BF16)
    rest = wg - wg1.astype(F32)
    wg2 = rest.astype(BF16)
    wg3 = (rest - wg2.astype(F32)).astype(BF16)
    logits = _dot(glr, wg1) + _dot(glr, wg2) + _dot(glr, wg3) + bg_ref[...]
    la = jax.nn.log_sigmoid(logits) / GLA_GATE_TAU
    hi, lo = _split_bf16(la)
    cum = _dot(t_ref[...], hi) + _dot(t_ref[...], lo)
    total = cum[c - 1:c, :]
    f_in = jnp.exp(cum)
    f_out = jnp.exp(total - cum)
    f_lev = [jnp.exp(-jnp.abs(cum - _gla_boundary_rows(cum, c >> (lev + 1)))) for lev in range(nlev)]

    q = q_ref[...].astype(F32) * (dk ** -0.5)
    k = k_ref[...].astype(F32)
    qs = [(q * f).astype(BF16) for f in f_lev] + [q.astype(BF16)]
    ks = [(k * f).astype(BF16) for f in f_lev] + [k.astype(BF16)]
    q_in = (q * f_in).astype(BF16)
    k_out = (k * f_out).astype(BF16)
    ones = jnp.ones((c, 2 * dv), BF16)
    head_a = lax.broadcasted_iota(jnp.int32, (c, 2 * dk), 1) < dk
    zero_k = jnp.zeros((c, 2 * dk), BF16)
    zero_v = jnp.zeros((c, dv), BF16)
    owned = [m_ref[lev * c:(lev + 1) * c, :] > 0.5 for lev in range(nlev + 1)]

    for p in range(npair):
        lanes = slice(p * 2 * dk, (p + 1) * 2 * dk)
        wide = slice(p * 2 * dv, (p + 1) * 2 * dv)
        scores = jnp.zeros((c, 2 * c), F32)
        for lev in range(nlev + 1):
            kp = ks[lev][:, lanes]
            kstack = jnp.concatenate([jnp.where(head_a, kp, zero_k), jnp.where(head_a, zero_k, kp)], axis=0)
            scores = jnp.where(owned[lev], _dot_nt(qs[lev][:, lanes], kstack), scores)
        v = v_ref[:, wide]
        v_bd = jnp.concatenate([jnp.concatenate([v[:, :dv], zero_v], axis=1),
                                jnp.concatenate([zero_v, v[:, dv:]], axis=1)], axis=0)
        st = st_ref[p]
        o = _dot(scores.astype(BF16), v_bd) + _dot(q_in[:, lanes], st.astype(BF16))
        tot_col = _dot_tn(hi[:, lanes], ones) + _dot_tn(lo[:, lanes], ones)
        st_ref[p] = jnp.exp(tot_col) * st + bm_ref[...] * _dot_tn(k_out[:, lanes], v)
        for hd in range(2):
            oh = o[:, hd * dv:(hd + 1) * dv]
            oh = oh * lax.rsqrt(jnp.mean(oh * oh, axis=-1, keepdims=True) + NORM_EPS)
            cols = slice(p * 2 * dv + hd * dv, p * 2 * dv + (hd + 1) * dv)
            rr = r_ref[:, cols].astype(F32)
            o_ref[:, cols] = (oh * gain_ref[:, cols] * (rr * jax.nn.sigmoid(rr))).astype(o_ref.dtype)


def gla_attention(proj, w_gate_pad, b_gate, gain, batch, seq, col0, c=GLA_CHUNK):
    dk, dv, nh = GLA_DK, GLA_DV, GLA_HEADS
    nc = seq // c
    nlev = _gla_levels(c)
    kw, vw = nh * dk, nh * dv
    qb = col0 // kw
    vb = (col0 + 2 * kw) // vw
    lrb = (col0 + 2 * kw + 2 * vw) // LANES
    tincl = jnp.asarray(np.tril(np.ones((c, c), np.float32)), dtype=BF16)
    pair_blocks = jnp.asarray(np.kron(np.eye(2, dtype=np.float32), np.ones((dk, dv), np.float32)))
    kern = functools.partial(_gla_kernel, c=c, nlev=nlev)
    row = lambda b, n: b * nc + n
    const = lambda shape: pl.BlockSpec(shape, lambda b, n: (0, 0))
    return pl.pallas_call(
        kern,
        grid=(batch, nc),
        in_specs=[pl.BlockSpec((c, kw), lambda b, n: (row(b, n), qb)),
                  pl.BlockSpec((c, kw), lambda b, n: (row(b, n), qb + 1)),
                  pl.BlockSpec((c, vw), lambda b, n: (row(b, n), vb)),
                  pl.BlockSpec((c, vw), lambda b, n: (row(b, n), vb + 1)),
                  pl.BlockSpec((c, LANES), lambda b, n: (row(b, n), lrb)),
                  const((LANES, kw)), const((1, kw)), const((1, vw)),
                  const((c, c)), const(((nlev + 1) * c, 2 * c)), const((2 * dk, 2 * dv))],
        out_specs=pl.BlockSpec((c, vw), lambda b, n: (row(b, n), 0)),
        out_shape=jax.ShapeDtypeStruct((batch * seq, vw), BF16),
        scratch_shapes=[pltpu.VMEM((nh // 2, 2 * dk, 2 * dv), F32)],
        compiler_params=_cparams(("parallel", "arbitrary")),
        name="gla",
    )(proj, proj, proj, proj, proj, w_gate_pad, b_gate.reshape(1, -1), gain.reshape(1, -1),
      tincl, _gla_masks(c), pair_blocks)


def _s5_prep_kernel(lr_ref, li_ref, ls_ref, bre_ref, bim_ref, cre_ref, cim_ref,
                    tin_ref, tout_ref, toep_ref, lam_ref, *, gb, r):
    h = S5_GROUP
    r16 = r * h
    first = lax.broadcasted_iota(jnp.int32, (1, LANES), 1) < S5_STATE
    npow = -(-(r + 1) // 8) * 8
    kidx = lax.broadcasted_iota(jnp.int32, (npow, LANES), 0).astype(F32)
    lane_group = (lax.broadcasted_iota(jnp.int32, (h, r * LANES), 1) % LANES) // h

    def tile_rows(x):
        return jnp.broadcast_to(x[None], (r, h, LANES)).reshape(r16, LANES)

    def by_token(x):
        return x.reshape(r, h, LANES).astype(BF16)

    def rows_of(table, powers):
        return jnp.concatenate([jnp.broadcast_to(table[k:k + 1, :], (h, LANES)) for k in powers], axis=0)

    lam_r, lam_i = [], []
    for gi in range(gb):
        lr = lr_ref[gi:gi + 1, :]
        li = li_ref[gi:gi + 1, :]
        dt = jnp.exp(ls_ref[gi:gi + 1, :])
        mag = jnp.exp(kidx * (dt * lr))
        ang = kidx * (dt * li)
        pw_r, pw_i = mag * jnp.cos(ang), mag * jnp.sin(ang)
        lbr, lbi = pw_r[1:2, :], pw_i[1:2, :]
        den = lr * lr + li * li
        nr, ni = lbr - 1.0, lbi
        cr = (nr * lr + ni * li) / den
        ci = (ni * lr - nr * li) / den
        bre, bim = bre_ref[gi], bim_ref[gi]
        bbr = cr * bre - ci * bim
        bbi = cr * bim + ci * bre
        pr, pi = rows_of(pw_r, range(r + 1)), rows_of(pw_i, range(r + 1))
        prr, pir = rows_of(pw_r, range(r - 1, -1, -1)), rows_of(pw_i, range(r - 1, -1, -1))
        bbr_t, bbi_t = tile_rows(bbr), tile_rows(bbi)
        tin_ref[0, :, gi] = by_token(prr * bbr_t - pir * bbi_t)
        tin_ref[1, :, gi] = by_token(prr * bbi_t + pir * bbr_t)
        cr_t, ci_t = tile_rows(cre_ref[gi]), tile_rows(cim_ref[gi])
        pr1, pi1 = pr[h:], pi[h:]
        tout_ref[0, :, gi] = by_token(cr_t * pr1 - ci_t * pi1)
        tout_ref[1, :, gi] = by_token(-(cr_t * pi1 + ci_t * pr1))
        pr0, pi0 = pr[:r16], pi[:r16]
        wk = jnp.where(first, cr_t * pr0 - ci_t * pi0, -(cr_t * pi0 + ci_t * pr0))
        wk_t = jnp.broadcast_to(wk.reshape(r, 1, h, LANES), (r, gb, h, LANES)).reshape(r * gb * h, LANES)
        bb = jnp.where(first, bbr, bbi)
        mt = _dot_nt(bb.astype(BF16), wk_t.astype(BF16))
        toep_ref[gi * h:(gi + 1) * h, :] = jnp.where(lane_group == gi, mt, 0.0).astype(toep_ref.dtype)
        lam_r.append(pr[r * h:r * h + 1, :])
        lam_i.append(pi[r * h:r * h + 1, :])

    def pairs(rows):
        return jnp.concatenate([jnp.where(first, rows[2 * q], rows[2 * q + 1]) for q in range(gb // 2)], axis=1)

    lam_ref[0:1, :] = pairs(lam_r)
    lam_ref[1:2, :] = pairs(lam_i)


def s5_prep(lam_re, lam_im, log_step, b_re, b_im, c_re, c_im, r=S5_R, gb=S5_GB):
    g = lam_re.shape[0]
    h = S5_GROUP
    nlb = g // gb
    dbl = lambda x: jnp.concatenate([x, x], axis=-1)
    lr2, li2 = dbl(lam_re), dbl(lam_im)
    ls2 = jnp.broadcast_to(log_step[:, None], (g, LANES))
    bre2 = dbl(jnp.swapaxes(b_re, 1, 2))
    bim2 = dbl(jnp.swapaxes(b_im, 1, 2))
    cre2, cim2 = dbl(c_re), dbl(c_im)
    vec = pl.BlockSpec((gb, LANES), lambda i: (i, 0))
    mat = pl.BlockSpec((gb, h, LANES), lambda i: (i, 0, 0))
    tspec = pl.BlockSpec((None, 2, r, gb, h, LANES), lambda i: (i, 0, 0, 0, 0, 0))
    tshape = jax.ShapeDtypeStruct((nlb, 2, r, gb, h, LANES), BF16)
    kern = functools.partial(_s5_prep_kernel, gb=gb, r=r)
    tin, tout, toep, lam = pl.pallas_call(
        kern,
        grid=(nlb,),
        in_specs=[vec, vec, vec, mat, mat, mat, mat],
        out_specs=[tspec, tspec,
                   pl.BlockSpec((None, gb * h, r * LANES), lambda i: (i, 0, 0)),
                   pl.BlockSpec((None, 2, gb // 2 * LANES), lambda i: (i, 0, 0))],
        out_shape=[tshape, tshape,
                   jax.ShapeDtypeStruct((nlb, gb * h, r * LANES), BF16),
                   jax.ShapeDtypeStruct((nlb, 2, gb // 2 * LANES), F32)],
        compiler_params=_cparams(("parallel",)),
        name="s5_prep",
    )(lr2, li2, ls2, bre2, bim2, cre2, cim2)
    rows = r * gb * h
    return tin.reshape(nlb, 2, rows, LANES), tout.reshape(nlb, 2, rows, LANES), toep, lam


def _s5_pair_mask(r, gb):
    group = (np.arange(r * LANES) % LANES) // S5_GROUP
    lane = np.arange(gb // 2 * LANES)
    target = 2 * (lane // LANES) + (lane % LANES) // S5_STATE
    return jnp.asarray((group[:, None] == target[None, :]).astype(np.float32), dtype=BF16)


def _s5_kernel(a_ref, tin_ref, tout_ref, toep_ref, lam_ref, mask_ref, d_ref, y_ref,
               tok_scr, in_scr, out_scr, bdin_scr, bdout_scr, bdt_scr, s_scr, xp_scr, *, batch, jn, r, gb):
    bj = batch * jn
    npair = gb // 2
    half = npair * LANES
    hop = S5_HOP
    assert r == hop * hop
    tok_scr[...] = a_ref[...].astype(F32)
    for c in range(hop):
        in_scr[c] = tok_scr[pl.ds(c, bj * hop, stride=hop), :]

    def a_tok(i):
        return in_scr[i % hop, pl.ds(i // hop, bj, stride=hop), :]

    for q in range(npair):
        mq = mask_ref[:, q * LANES:(q + 1) * LANES]
        for part in range(2):
            cols = slice((part * npair + q) * LANES, (part * npair + q + 1) * LANES)
            bdin_scr[:, cols] = tin_ref[part] * mq
            bdout_scr[:, cols] = tout_ref[part] * mq
    for i in range(r):
        if i > 0:
            bdt_scr[i * LANES:(i + 1) * LANES, :i * LANES] = jnp.zeros((LANES, i * LANES), bdt_scr.dtype)
        bdt_scr[i * LANES:(i + 1) * LANES, i * LANES:] = toep_ref[:, :(r - i) * LANES]

    a = jnp.concatenate([a_tok(i).astype(BF16) for i in range(r)], axis=1)
    s_scr[...] = _dot(a, bdin_scr[...])
    lam_r = lam_ref[0:1, :]
    lam_i = lam_ref[1:2, :]

    def step(j, carry):
        out = []
        for b in range(batch):
            xr, xi = carry[2 * b], carry[2 * b + 1]
            row = pl.ds(b * jn + j, 1)
            xp_scr[row, :half] = xr
            xp_scr[row, half:] = xi
            s = s_scr[row, :]
            out.append(lam_r * xr - lam_i * xi + s[:, :half])
            out.append(lam_r * xi + lam_i * xr + s[:, half:])
        return tuple(out)

    zero = jnp.zeros((1, half), F32)
    lax.fori_loop(0, jn, step, (zero,) * (2 * batch))

    y_state = _dot_nt(xp_scr[...].astype(BF16), bdout_scr[...])
    d = d_ref[...]
    span = r // S5_TOEP_SPLIT
    for g in range(S5_TOEP_SPLIT):
        rows = (g + 1) * span * LANES
        cols = slice(g * span * LANES, rows)
        y = y_state[:, cols] + _dot(a[:, :rows], bdt_scr[:rows, cols])
        for i in range(g * span, (g + 1) * span):
            yi = y[:, (i - g * span) * LANES:(i - g * span + 1) * LANES] + a_tok(i) * d
            out_scr[i % hop, pl.ds(i // hop, bj, stride=hop), :] = 0.5 * yi * (1.0 + lax.erf(yi * (2.0 ** -0.5)))
    for c in range(hop):
        tok_scr[pl.ds(c, bj * hop, stride=hop), :] = out_scr[c]
    y_ref[...] = tok_scr[...].astype(y_ref.dtype)


def s5_mixer_gelu(a_norm, batch, seq, lam_re, lam_im, log_step, b_re, b_im, c_re, c_im, d_skip,
                  r=S5_R, gb=S5_GB):
    m, d = a_norm.shape
    assert gb * S5_GROUP == LANES
    nlb = d // LANES
    jn = seq // r
    bj = batch * jn
    half = gb // 2 * LANES
    tin, tout, toep, lam = s5_prep(lam_re, lam_im, log_step, b_re, b_im, c_re, c_im, r=r, gb=gb)
    lane_block = pl.BlockSpec((m, LANES), lambda lb: (0, lb))
    whole = lambda *shape: pl.BlockSpec((None,) + shape, lambda lb: (lb,) + (0,) * len(shape))
    kern = functools.partial(_s5_kernel, batch=batch, jn=jn, r=r, gb=gb)
    return pl.pallas_call(
        kern,
        grid=(nlb,),
        in_specs=[lane_block,
                  whole(2, r * LANES, LANES), whole(2, r * LANES, LANES), whole(LANES, r * LANES), whole(2, half),
                  pl.BlockSpec((r * LANES, half), lambda lb: (0, 0)),
                  pl.BlockSpec((1, LANES), lambda lb: (0, lb))],
        out_specs=lane_block,
        out_shape=jax.ShapeDtypeStruct((m, d), BF16),
        scratch_shapes=[pltpu.VMEM((m, LANES), F32),
                        pltpu.VMEM((S5_HOP, m // S5_HOP, LANES), F32), pltpu.VMEM((S5_HOP, m // S5_HOP, LANES), F32),
                        pltpu.VMEM((r * LANES, 2 * half), BF16), pltpu.VMEM((r * LANES, 2 * half), BF16),
                        pltpu.VMEM((r * LANES, r * LANES), BF16),
                        pltpu.VMEM((bj, 2 * half), F32), pltpu.VMEM((bj, 2 * half), F32)],
        compiler_params=_cparams(("parallel",)),
        name="s5_blocks",
    )(a_norm, tin, tout, toep, lam, _s5_pair_mask(r, gb), d_skip.reshape(1, d))


def kernel(x, norm_gains, w_in, w_gate_up, b_gate, gla_norm_gain, w_out, s5_lambda_re, s5_lambda_im,
           s5_log_step, s5_b_re, s5_b_im, s5_c_re, s5_c_im, s5_d, w_glu, w_ffn_in, w_ffn_out):
    batch, seq, d = x.shape
    m = batch * seq
    depth = norm_gains.shape[0]
    sb_w = SB_HEADS * SB_HEAD_DIM

    proj_w = -(-w_in.shape[2] // PROJ_TN) * PROJ_TN
    w_in_b = jnp.concatenate(
        [w_in.astype(BF16), jnp.zeros(w_in.shape[:2] + (proj_w - w_in.shape[2],), BF16)], axis=2)
    w_out_b, w_glu_b, w_ffn_out_b = w_out.astype(BF16), w_glu.astype(BF16), w_ffn_out.astype(BF16)

    h = x.astype(F32).reshape(m, d)
    a = rms_norm_bf16(h, norm_gains[0, 0])
    for layer in range(depth):
        gains = norm_gains[layer]
        i = layer // 2
        if layer % 2 == 0:
            proj = matmul(a, w_in_b, i, proj_w, BF16, tm=2048, tn=PROJ_TN)
            o_sb = sb_attention(proj, batch, seq)
            w_gate_pad = jnp.pad(w_gate_up[i], ((0, LANES - GLA_GATE_RANK), (0, 0)))
            o_gla = gla_attention(proj, w_gate_pad, b_gate[i], gla_norm_gain[i], batch, seq, 3 * sb_w)
            h, a = matmul_residual((o_sb, o_gla), w_out_b, i, h, gains[1], gains[2])
        else:
            y = s5_mixer_gelu(a, batch, seq, s5_lambda_re[i], s5_lambda_im[i], s5_log_step[i],
                              s5_b_re[i], s5_b_im[i], s5_c_re[i], s5_c_im[i], s5_d[i])
            h, a = matmul_residual(y, w_glu_b, i, h, gains[1], gains[2], glu=True, tm=1024, tk=d // 4,
                                   manual_h=True)
        f = ffn_in(a, w_ffn_in, layer)
        g_next = norm_gains[layer + 1, 0] if layer + 1 < depth else gains[3]
        h, a = matmul_residual(f, w_ffn_out_b, layer, h, gains[3], g_next, tm=1024,
                               tk=w_ffn_out.shape[1] // 4, manual_h=True)
    return h.reshape(batch, seq, d).astype(x.dtype)
```

```python
import functools
import math

import numpy as np
import jax
import jax.numpy as jnp
from jax import lax
from jax.experimental import pallas as pl
from jax.experimental.pallas import tpu as pltpu

F32 = jnp.float32
BF16 = jnp.bfloat16

NORM_EPS = 1e-6
LANES = 128
SB_HEADS = 8
SB_HEAD_DIM = 128
GLA_HEADS = 8
GLA_DK = 64
GLA_DV = 128
GLA_GATE_RANK = 16
GLA_GATE_TAU = 16.0
S5_GROUP = 16
S5_STATE = 64

SB_TQ = 1024
SB_TK = 256
GLA_CHUNK = 128
PROJ_TN = 1280
S5_R = 16
S5_GB = 8
S5_TOEP_SPLIT = 4
S5_HOP = 4

VMEM_LIMIT = 56 * 1024 * 1024

LOG2E = math.log2(math.e)


def _cparams(sem):
    return pltpu.CompilerParams(dimension_semantics=sem, vmem_limit_bytes=VMEM_LIMIT)


def _rms(x, g):
    return x * lax.rsqrt(jnp.mean(x * x, axis=-1, keepdims=True) + NORM_EPS) * g


def _dot(a, b):
    return jnp.dot(a, b, preferred_element_type=F32)


def _dot_nt(a, b):
    return lax.dot_general(a, b, (((1,), (1,)), ((), ())), preferred_element_type=F32)


def _dot_tn(a, b):
    return lax.dot_general(a, b, (((0,), (0,)), ((), ())), preferred_element_type=F32)


def _split_bf16(x):
    hi = x.astype(BF16)
    lo = (x - hi.astype(F32)).astype(BF16)
    return hi, lo


def _norm_kernel(h_ref, g_ref, o_ref):
    o_ref[...] = _rms(h_ref[...], g_ref[...]).astype(o_ref.dtype)


def rms_norm_bf16(h, g, tm=512):
    m, d = h.shape
    return pl.pallas_call(
        _norm_kernel,
        grid=(m // tm,),
        in_specs=[pl.BlockSpec((tm, d), lambda i: (i, 0)),
                  pl.BlockSpec((1, d), lambda i: (0, 0))],
        out_specs=pl.BlockSpec((tm, d), lambda i: (i, 0)),
        out_shape=jax.ShapeDtypeStruct((m, d), BF16),
        compiler_params=_cparams(("parallel",)),
        name="rms_norm",
    )(h, g.reshape(1, d))


def _mm_kernel(a_ref, w_ref, o_ref, wb_ref):
    @pl.when(pl.program_id(1) == 0)
    def _():
        wb_ref[...] = w_ref[...].astype(BF16)

    o_ref[...] = _dot(a_ref[...], wb_ref[...]).astype(o_ref.dtype)


def matmul(a, w, layer, ncols, out_dtype, tm, tn):
    m, k = a.shape
    return pl.pallas_call(
        _mm_kernel,
        grid=(ncols // tn, m // tm),
        in_specs=[pl.BlockSpec((tm, k), lambda j, i: (i, 0)),
                  pl.BlockSpec((None, k, tn), lambda j, i: (layer, 0, j))],
        out_specs=pl.BlockSpec((tm, tn), lambda j, i: (i, j)),
        out_shape=jax.ShapeDtypeStruct((m, ncols), out_dtype),
        scratch_shapes=[pltpu.VMEM((k, tn), BF16)],
        compiler_params=_cparams(("parallel", "arbitrary")),
        name="matmul",
    )(a, w)


def _ffn_in_kernel(a_ref, wg_ref, wu_ref, o_ref, wgb_ref, wub_ref):
    @pl.when(pl.program_id(1) == 0)
    def _():
        wgb_ref[...] = wg_ref[...].astype(BF16)
        wub_ref[...] = wu_ref[...].astype(BF16)

    a = a_ref[...]
    g = _dot(a, wgb_ref[...])
    u = _dot(a, wub_ref[...])
    o_ref[...] = (g * jax.nn.sigmoid(g) * u).astype(o_ref.dtype)


def ffn_in(a, w, layer, tm=2048, tn=512):
    m, k = a.shape
    nf = w.shape[2] // 2
    nj = nf // tn
    return pl.pallas_call(
        _ffn_in_kernel,
        grid=(nj, m // tm),
        in_specs=[pl.BlockSpec((tm, k), lambda j, i: (i, 0)),
                  pl.BlockSpec((None, k, tn), lambda j, i: (layer, 0, j)),
                  pl.BlockSpec((None, k, tn), lambda j, i: (layer, 0, j + nj))],
        out_specs=pl.BlockSpec((tm, tn), lambda j, i: (i, j)),
        out_shape=jax.ShapeDtypeStruct((m, nf), BF16),
        scratch_shapes=[pltpu.VMEM((k, tn), BF16), pltpu.VMEM((k, tn), BF16)],
        compiler_params=_cparams(("parallel", "arbitrary")),
        name="ffn_in",
    )(a, w, w)


def _mm_res_kernel(*refs, nk, glu, d, n_a, tm, manual_h):
    a_refs = refs[:n_a]
    w_ref, h_ref, gpost_ref, gnext_ref, hout_ref, anext_ref, acc_ref = refs[n_a:n_a + 7]
    i = pl.program_id(0)
    k = pl.program_id(1)

    def product():
        if n_a == 1:
            return _dot(a_refs[0][...], w_ref[...])
        out, lo = None, 0
        for ar in a_refs:
            part = _dot(ar[...], w_ref[lo:lo + ar.shape[1], :])
            out = part if out is None else out + part
            lo += ar.shape[1]
        return out

    def finish(h_tile):
        y = acc_ref[...]
        if glu:
            y = y[:, :d] * jax.nn.sigmoid(y[:, d:])
        hn = h_tile + _rms(y, gpost_ref[...])
        anext_ref[...] = _rms(hn, gnext_ref[...]).astype(anext_ref.dtype)
        return hn

    @pl.when(k == 0)
    def _():
        acc_ref[...] = product()

    @pl.when(k > 0)
    def _():
        acc_ref[...] += product()

    if not manual_h:
        @pl.when(k == nk - 1)
        def _():
            hout_ref[...] = finish(h_ref[...])
        return

    hbuf, sems = refs[n_a + 7:]
    nt = pl.num_programs(0)

    def tile_in(t):
        return pltpu.make_async_copy(h_ref.at[pl.ds(pl.multiple_of(t * tm, tm), tm), :], hbuf, sems.at[0])

    def tile_out(t):
        return pltpu.make_async_copy(hbuf, hout_ref.at[pl.ds(pl.multiple_of(t * tm, tm), tm), :], sems.at[1])

    @pl.when(k == 1)
    def _():
        @pl.when(i > 0)
        def _():
            tile_out(i - 1).wait()
        tile_in(i).start()

    @pl.when(k == nk - 1)
    def _():
        tile_in(i).wait()
        hbuf[...] = finish(hbuf[...])
        tile_out(i).start()

        @pl.when(i == nt - 1)
        def _():
            tile_out(i).wait()


def matmul_residual(a, w, layer, h, g_post, g_next, *, glu=False, tm=512, tk=512, manual_h=False):
    a_list = list(a) if isinstance(a, (tuple, list)) else [a]
    n_a = len(a_list)
    m = a_list[0].shape[0]
    kdim, n = w.shape[1:]
    d = h.shape[1]
    if n_a == 1:
        a_specs = [pl.BlockSpec((tm, tk), lambda i, k: (i, k))]
    else:
        tk = kdim
        assert sum(x.shape[1] for x in a_list) == kdim
        a_specs = [pl.BlockSpec((tm, x.shape[1]), lambda i, k: (i, 0)) for x in a_list]
    nk = kdim // tk
    assert nk >= 2 or not manual_h
    kern = functools.partial(_mm_res_kernel, nk=nk, glu=glu, d=d, n_a=n_a, tm=tm, manual_h=manual_h)
    h_spec = pl.BlockSpec(memory_space=pl.ANY) if manual_h else pl.BlockSpec((tm, d), lambda i, k: (i, 0))
    scratch = [pltpu.VMEM((tm, n), F32)]
    if manual_h:
        scratch += [pltpu.VMEM((tm, d), F32), pltpu.SemaphoreType.DMA((2,))]
    return pl.pallas_call(
        kern,
        grid=(m // tm, nk),
        in_specs=a_specs + [
                  pl.BlockSpec((None, tk, n), lambda i, k: (layer, k, 0)),
                  h_spec,
                  pl.BlockSpec((1, d), lambda i, k: (0, 0)),
                  pl.BlockSpec((1, d), lambda i, k: (0, 0))],
        out_specs=[h_spec, pl.BlockSpec((tm, d), lambda i, k: (i, 0))],
        out_shape=[jax.ShapeDtypeStruct((m, d), F32),
                   jax.ShapeDtypeStruct((m, d), BF16)],
        scratch_shapes=scratch,
        compiler_params=_cparams(("arbitrary" if manual_h else "parallel", "arbitrary")),
        name="matmul_residual",
    )(*a_list, w, h, g_post.reshape(1, d), g_next.reshape(1, d))


def _sb_kernel(q_ref, k_ref, v_ref, u_ref, o_ref, acc_ref, c_ref, *, tq, tk, scale):
    i = pl.program_id(2)
    q = (q_ref[...].astype(F32) * scale).astype(BF16)
    u = u_ref[...]
    acc_ref[...] = jnp.zeros_like(acc_ref)
    c_ref[...] = jnp.zeros_like(c_ref)

    diag_keep = (lax.broadcasted_iota(jnp.int32, (tk, tk), 0) <
                 lax.broadcasted_iota(jnp.int32, (tk, tk), 1))

    def run(items):
        def rows(ref, it):
            return ref[pl.ds(pl.multiple_of(it[0] * tk, tk), tk), :]

        def logits(it):
            return _dot_nt(rows(k_ref, it), q[it[2], :])

        def suffix(it, z):
            ls = jnp.minimum(z, 0.0) - jnp.log(1.0 + jnp.exp2(jnp.abs(z) * (-LOG2E)))
            lk = ls - z
            if it[1]:
                lk = jnp.where(diag_keep, lk, 0.0)
            later = _dot(u, lk.astype(BF16))
            return ls, later, later[0:1, :] + lk[0:1, :]

        def weighted(it, ls, later):
            w = jnp.exp(ls + later)
            if it[1]:
                w = jnp.where(diag_keep, w, 0.0)
            return _dot_tn(rows(v_ref, it), w.astype(BF16))

        zs = [logits(it) for it in items]
        sfx = [None] * len(items)
        pvs = [None] * len(items)
        sfx[0] = suffix(items[0], zs[0])
        for n in range(1, len(items)):
            sfx[n] = suffix(items[n], zs[n])
            pvs[n - 1] = weighted(items[n - 1], sfx[n - 1][0], sfx[n - 1][1])
        pvs[-1] = weighted(items[-1], sfx[-1][0], sfx[-1][1])
        for it, (_, _, total), pv in zip(items, sfx, pvs):
            acc_ref[:, it[2]] += jnp.exp(c_ref[:, it[2]]) * pv
            c_ref[:, it[2]] += total

    nstraddle = tq // tk
    items = []
    for s in range(nstraddle - 1, -1, -1):
        items.append((i * nstraddle + s, True, slice(s * tk, (s + 1) * tk)))
        if s + 1 < nstraddle:
            items.append((i * nstraddle + s, False, slice((s + 1) * tk, tq)))
    run(items)

    def full_blocks(first, count):
        run([(first - n, False, slice(0, tq)) for n in range(count)])

    nfull = i * nstraddle
    group = 2 * nstraddle

    @pl.when(nfull % group != 0)
    def _():
        full_blocks(nfull - 1, nstraddle)

    def body(n, carry):
        full_blocks(nfull - nfull % group - 1 - group * n, group)
        return carry

    lax.fori_loop(0, nfull // group, body, 0)
    o_ref[...] = acc_ref[...].T.astype(o_ref.dtype)


def _sb_umat(tk):
    s = np.arange(tk)[:, None]
    j = np.arange(tk)[None, :]
    return jnp.asarray((j > s).astype(np.float32), dtype=BF16)


def sb_attention(proj, batch, seq, tq=SB_TQ, tk=SB_TK):
    d, nh = SB_HEAD_DIM, SB_HEADS
    assert (tq // tk) % 2 == 0
    nq = seq // tq
    kern = functools.partial(_sb_kernel, tq=tq, tk=tk, scale=d ** -0.5)
    return pl.pallas_call(
        kern,
        grid=(batch, nh, nq),
        in_specs=[pl.BlockSpec((tq, d), lambda b, h, i: (b * nq + i, h)),
                  pl.BlockSpec((seq, d), lambda b, h, i: (b, nh + h)),
                  pl.BlockSpec((seq, d), lambda b, h, i: (b, 2 * nh + h)),
                  pl.BlockSpec((tk, tk), lambda b, h, i: (0, 0))],
        out_specs=pl.BlockSpec((tq, d), lambda b, h, i: (b * nq + i, h)),
        out_shape=jax.ShapeDtypeStruct((batch * seq, nh * d), BF16),
        scratch_shapes=[pltpu.VMEM((d, tq), F32), pltpu.VMEM((1, tq), F32)],
        compiler_params=_cparams(("parallel", "parallel", "parallel")),
        name="sb_attention",
    )(proj, proj, proj, _sb_umat(tk))


def _gla_levels(c):
    return int(math.log2(c))


def _gla_masks(c):
    t = np.arange(c)[:, None]
    s = np.arange(c)[None, :]
    masks = []
    for lev in range(_gla_levels(c)):
        half = c >> (lev + 1)
        blk = 2 * half
        masks.append((t // blk == s // blk) & ((t % blk) >= half) & ((s % blk) < half))
    masks.append(t == s)
    m = np.concatenate(masks, axis=0).astype(np.float32)
    return jnp.asarray(np.concatenate([m, m], axis=1))


def _gla_boundary_rows(cum, half):
    c, width = cum.shape
    blk = 2 * half
    sub = 8
    if half >= sub:
        return jnp.concatenate(
            [jnp.broadcast_to(cum[b * blk + half - 1:b * blk + half, :], (blk, width)) for b in range(c // blk)],
            axis=0)
    x = cum.reshape(c // sub, sub, width)
    row = lax.broadcasted_iota(jnp.int32, (c // sub, sub, width), 1)
    out = None
    for b in range(sub // blk):
        piece = jnp.broadcast_to(x[:, b * blk + half - 1:b * blk + half, :], x.shape)
        out = piece if out is None else jnp.where(row >= b * blk, piece, out)
    return out.reshape(c, width)


def _gla_kernel(q_ref, k_ref, v_ref, r_ref, glr_ref, wg_ref, bg_ref, gain_ref, t_ref, m_ref, bm_ref,
                o_ref, st_ref, *, c, nlev):
    @pl.when(pl.program_id(1) == 0)
    def _():
        st_ref[...] = jnp.zeros_like(st_ref)

    dk, dv = GLA_DK, GLA_DV
    npair = GLA_HEADS // 2
    glr = glr_ref[...]
    wg = wg_ref[...]
    wg1 = wg.astype(BF16)
    rest = wg - wg1.astype(F32)
    wg2 = rest.astype(BF16)
    wg3 = (rest - wg2.astype(F32)).astype(BF16)
    logits = _dot(glr, wg1) + _dot(glr, wg2) + _dot(glr, wg3) + bg_ref[...]
    la = jax.nn.log_sigmoid(logits) / GLA_GATE_TAU
    hi, lo = _split_bf16(la)
    cum = _dot(t_ref[...], hi) + _dot(t_ref[...], lo)
    total = cum[c - 1:c, :]
    f_in = jnp.exp(cum)
    f_out = jnp.exp(total - cum)
    f_lev = [jnp.exp(-jnp.abs(cum - _gla_boundary_rows(cum, c >> (lev + 1)))) for lev in range(nlev)]

    q = q_ref[...].astype(F32) * (dk ** -0.5)
    k = k_ref[...].astype(F32)
    qs = [(q * f).astype(BF16) for f in f_lev] + [q.astype(BF16)]
    ks = [(k * f).astype(BF16) for f in f_lev] + [k.astype(BF16)]
    q_in = (q * f_in).astype(BF16)
    k_out = (k * f_out).astype(BF16)
    ones = jnp.ones((c, 2 * dv), BF16)
    head_a = lax.broadcasted_iota(jnp.int32, (c, 2 * dk), 1) < dk
    zero_k = jnp.zeros((c, 2 * dk), BF16)
    zero_v = jnp.zeros((c, dv), BF16)
    owned = [m_ref[lev * c:(lev + 1) * c, :] > 0.5 for lev in range(nlev + 1)]

    for p in range(npair):
        lanes = slice(p * 2 * dk, (p + 1) * 2 * dk)
        wide = slice(p * 2 * dv, (p + 1) * 2 * dv)
        scores = jnp.zeros((c, 2 * c), F32)
        for lev in range(nlev + 1):
            kp = ks[lev][:, lanes]
            kstack = jnp.concatenate([jnp.where(head_a, kp, zero_k), jnp.where(head_a, zero_k, kp)], axis=0)
            scores = jnp.where(owned[lev], _dot_nt(qs[lev][:, lanes], kstack), scores)
        v = v_ref[:, wide]
        v_bd = jnp.concatenate([jnp.concatenate([v[:, :dv], zero_v], axis=1),
                                jnp.concatenate([zero_v, v[:, dv:]], axis=1)], axis=0)
        st = st_ref[p]
        o = _dot(scores.astype(BF16), v_bd) + _dot(q_in[:, lanes], st.astype(BF16))
        tot_col = _dot_tn(hi[:, lanes], ones) + _dot_tn(lo[:, lanes], ones)
        st_ref[p] = jnp.exp(tot_col) * st + bm_ref[...] * _dot_tn(k_out[:, lanes], v)
        for hd in range(2):
            oh = o[:, hd * dv:(hd + 1) * dv]
            oh = oh * lax.rsqrt(jnp.mean(oh * oh, axis=-1, keepdims=True) + NORM_EPS)
            cols = slice(p * 2 * dv + hd * dv, p * 2 * dv + (hd + 1) * dv)
            rr = r_ref[:, cols].astype(F32)
            o_ref[:, cols] = (oh * gain_ref[:, cols] * (rr * jax.nn.sigmoid(rr))).astype(o_ref.dtype)


def gla_attention(proj, w_gate_pad, b_gate, gain, batch, seq, col0, c=GLA_CHUNK):
    dk, dv, nh = GLA_DK, GLA_DV, GLA_HEADS
    nc = seq // c
    nlev = _gla_levels(c)
    kw, vw = nh * dk, nh * dv
    qb = col0 // kw
    vb = (col0 + 2 * kw) // vw
    lrb = (col0 + 2 * kw + 2 * vw) // LANES
    tincl = jnp.asarray(np.tril(np.ones((c, c), np.float32)), dtype=BF16)
    pair_blocks = jnp.asarray(np.kron(np.eye(2, dtype=np.float32), np.ones((dk, dv), np.float32)))
    kern = functools.partial(_gla_kernel, c=c, nlev=nlev)
    row = lambda b, n: b * nc + n
    const = lambda shape: pl.BlockSpec(shape, lambda b, n: (0, 0))
    return pl.pallas_call(
        kern,
        grid=(batch, nc),
        in_specs=[pl.BlockSpec((c, kw), lambda b, n: (row(b, n), qb)),
                  pl.BlockSpec((c, kw), lambda b, n: (row(b, n), qb + 1)),
                  pl.BlockSpec((c, vw), lambda b, n: (row(b, n), vb)),
                  pl.BlockSpec((c, vw), lambda b, n: (row(b, n), vb + 1)),
                  pl.BlockSpec((c, LANES), lambda b, n: (row(b, n), lrb)),
                  const((LANES, kw)), const((1, kw)), const((1, vw)),
                  const((c, c)), const(((nlev + 1) * c, 2 * c)), const((2 * dk, 2 * dv))],
        out_specs=pl.BlockSpec((c, vw), lambda b, n: (row(b, n), 0)),
        out_shape=jax.ShapeDtypeStruct((batch * seq, vw), BF16),
        scratch_shapes=[pltpu.VMEM((nh // 2, 2 * dk, 2 * dv), F32)],
        compiler_params=_cparams(("parallel", "arbitrary")),
        name="gla",
    )(proj, proj, proj, proj, proj, w_gate_pad, b_gate.reshape(1, -1), gain.reshape(1, -1),
      tincl, _gla_masks(c), pair_blocks)


def _s5_prep_kernel(lr_ref, li_ref, ls_ref, bre_ref, bim_ref, cre_ref, cim_ref,
                    tin_ref, tout_ref, toep_ref, lam_ref, *, gb, r):
    h = S5_GROUP
    r16 = r * h
    first = lax.broadcasted_iota(jnp.int32, (1, LANES), 1) < S5_STATE
    npow = -(-(r + 1) // 8) * 8
    kidx = lax.broadcasted_iota(jnp.int32, (npow, LANES), 0).astype(F32)
    lane_group = (lax.broadcasted_iota(jnp.int32, (h, r * LANES), 1) % LANES) // h

    def tile_rows(x):
        return jnp.broadcast_to(x[None], (r, h, LANES)).reshape(r16, LANES)

    def by_token(x):
        return x.reshape(r, h, LANES).astype(BF16)

    def rows_of(table, powers):
        return jnp.concatenate([jnp.broadcast_to(table[k:k + 1, :], (h, LANES)) for k in powers], axis=0)

    lam_r, lam_i = [], []
    for gi in range(gb):
        lr = lr_ref[gi:gi + 1, :]
        li = li_ref[gi:gi + 1, :]
        dt = jnp.exp(ls_ref[gi:gi + 1, :])
        mag = jnp.exp(kidx * (dt * lr))
        ang = kidx * (dt * li)
        pw_r, pw_i = mag * jnp.cos(ang), mag * jnp.sin(ang)
        lbr, lbi = pw_r[1:2, :], pw_i[1:2, :]
        den = lr * lr + li * li
        nr, ni = lbr - 1.0, lbi
        cr = (nr * lr + ni * li) / den
        ci = (ni * lr - nr * li) / den
        bre, bim = bre_ref[gi], bim_ref[gi]
        bbr = cr * bre - ci * bim
        bbi = cr * bim + ci * bre
        pr, pi = rows_of(pw_r, range(r + 1)), rows_of(pw_i, range(r + 1))
        prr, pir = rows_of(pw_r, range(r - 1, -1, -1)), rows_of(pw_i, range(r - 1, -1, -1))
        bbr_t, bbi_t = tile_rows(bbr), tile_rows(bbi)
        tin_ref[0, :, gi] = by_token(prr * bbr_t - pir * bbi_t)
        tin_ref[1, :, gi] = by_token(prr * bbi_t + pir * bbr_t)
        cr_t, ci_t = tile_rows(cre_ref[gi]), tile_rows(cim_ref[gi])
        pr1, pi1 = pr[h:], pi[h:]
        tout_ref[0, :, gi] = by_token(cr_t * pr1 - ci_t * pi1)
        tout_ref[1, :, gi] = by_token(-(cr_t * pi1 + ci_t * pr1))
        pr0, pi0 = pr[:r16], pi[:r16]
        wk = jnp.where(first, cr_t * pr0 - ci_t * pi0, -(cr_t * pi0 + ci_t * pr0))
        wk_t = jnp.broadcast_to(wk.reshape(r, 1, h, LANES), (r, gb, h, LANES)).reshape(r * gb * h, LANES)
        bb = jnp.where(first, bbr, bbi)
        mt = _dot_nt(bb.astype(BF16), wk_t.astype(BF16))
        toep_ref[gi * h:(gi + 1) * h, :] = jnp.where(lane_group == gi, mt, 0.0).astype(toep_ref.dtype)
        lam_r.append(pr[r * h:r * h + 1, :])
        lam_i.append(pi[r * h:r * h + 1, :])

    def pairs(rows):
        return jnp.concatenate([jnp.where(first, rows[2 * q], rows[2 * q + 1]) for q in range(gb // 2)], axis=1)

    lam_ref[0:1, :] = pairs(lam_r)
    lam_ref[1:2, :] = pairs(lam_i)


def s5_prep(lam_re, lam_im, log_step, b_re, b_im, c_re, c_im, r=S5_R, gb=S5_GB):
    g = lam_re.shape[0]
    h = S5_GROUP
    nlb = g // gb
    dbl = lambda x: jnp.concatenate([x, x], axis=-1)
    lr2, li2 = dbl(lam_re), dbl(lam_im)
    ls2 = jnp.broadcast_to(log_step[:, None], (g, LANES))
    bre2 = dbl(jnp.swapaxes(b_re, 1, 2))
    bim2 = dbl(jnp.swapaxes(b_im, 1, 2))
    cre2, cim2 = dbl(c_re), dbl(c_im)
    vec = pl.BlockSpec((gb, LANES), lambda i: (i, 0))
    mat = pl.BlockSpec((gb, h, LANES), lambda i: (i, 0, 0))
    tspec = pl.BlockSpec((None, 2, r, gb, h, LANES), lambda i: (i, 0, 0, 0, 0, 0))
    tshape = jax.ShapeDtypeStruct((nlb, 2, r, gb, h, LANES), BF16)
    kern = functools.partial(_s5_prep_kernel, gb=gb, r=r)
    tin, tout, toep, lam = pl.pallas_call(
        kern,
        grid=(nlb,),
        in_specs=[vec, vec, vec, mat, mat, mat, mat],
        out_specs=[tspec, tspec,
                   pl.BlockSpec((None, gb * h, r * LANES), lambda i: (i, 0, 0)),
                   pl.BlockSpec((None, 2, gb // 2 * LANES), lambda i: (i, 0, 0))],
        out_shape=[tshape, tshape,
                   jax.ShapeDtypeStruct((nlb, gb * h, r * LANES), BF16),
                   jax.ShapeDtypeStruct((nlb, 2, gb // 2 * LANES), F32)],
        compiler_params=_cparams(("parallel",)),
        name="s5_prep",
    )(lr2, li2, ls2, bre2, bim2, cre2, cim2)
    rows = r * gb * h
    return tin.reshape(nlb, 2, rows, LANES), tout.reshape(nlb, 2, rows, LANES), toep, lam


def _s5_pair_mask(r, gb):
    group = (np.arange(r * LANES) % LANES) // S5_GROUP
    lane = np.arange(gb // 2 * LANES)
    target = 2 * (lane // LANES) + (lane % LANES) // S5_STATE
    return jnp.asarray((group[:, None] == target[None, :]).astype(np.float32), dtype=BF16)


def _s5_kernel(a_ref, tin_ref, tout_ref, toep_ref, lam_ref, mask_ref, d_ref, y_ref,
               tok_scr, in_scr, out_scr, bdin_scr, bdout_scr, bdt_scr, s_scr, xp_scr, *, batch, jn, r, gb):
    bj = batch * jn
    npair = gb // 2
    half = npair * LANES
    hop = S5_HOP
    assert r == hop * hop
    tok_scr[...] = a_ref[...].astype(F32)
    for c in range(hop):
        in_scr[c] = tok_scr[pl.ds(c, bj * hop, stride=hop), :]

    def a_tok(i):
        return in_scr[i % hop, pl.ds(i // hop, bj, stride=hop), :]

    for q in range(npair):
        mq = mask_ref[:, q * LANES:(q + 1) * LANES]
        for part in range(2):
            cols = slice((part * npair + q) * LANES, (part * npair + q + 1) * LANES)
            bdin_scr[:, cols] = tin_ref[part] * mq
            bdout_scr[:, cols] = tout_ref[part] * mq
    for i in range(r):
        if i > 0:
            bdt_scr[i * LANES:(i + 1) * LANES, :i * LANES] = jnp.zeros((LANES, i * LANES), bdt_scr.dtype)
        bdt_scr[i * LANES:(i + 1) * LANES, i * LANES:] = toep_ref[:, :(r - i) * LANES]

    a = jnp.concatenate([a_tok(i).astype(BF16) for i in range(r)], axis=1)
    s_scr[...] = _dot(a, bdin_scr[...])
    lam_r = lam_ref[0:1, :]
    lam_i = lam_ref[1:2, :]

    def step(j, carry):
        out = []
        for b in range(batch):
            xr, xi = carry[2 * b], carry[2 * b + 1]
            row = pl.ds(b * jn + j, 1)
            xp_scr[row, :half] = xr
            xp_scr[row, half:] = xi
            s = s_scr[row, :]
            out.append(lam_r * xr - lam_i * xi + s[:, :half])
            out.append(lam_r * xi + lam_i * xr + s[:, half:])
        return tuple(out)

    zero = jnp.zeros((1, half), F32)
    lax.fori_loop(0, jn, step, (zero,) * (2 * batch), unroll=4)

    y_state = _dot_nt(xp_scr[...].astype(BF16), bdout_scr[...])
    d = d_ref[...]
    span = r // S5_TOEP_SPLIT
    for g in range(S5_TOEP_SPLIT):
        rows = (g + 1) * span * LANES
        cols = slice(g * span * LANES, rows)
        y = y_state[:, cols] + _dot(a[:, :rows], bdt_scr[:rows, cols])
        for i in range(g * span, (g + 1) * span):
            yi = y[:, (i - g * span) * LANES:(i - g * span + 1) * LANES] + a_tok(i) * d
            out_scr[i % hop, pl.ds(i // hop, bj, stride=hop), :] = 0.5 * yi * (1.0 + lax.erf(yi * (2.0 ** -0.5)))
    for c in range(hop):
        tok_scr[pl.ds(c, bj * hop, stride=hop), :] = out_scr[c]
    y_ref[...] = tok_scr[...].astype(y_ref.dtype)


def s5_mixer_gelu(a_norm, batch, seq, lam_re, lam_im, log_step, b_re, b_im, c_re, c_im, d_skip,
                  r=S5_R, gb=S5_GB):
    m, d = a_norm.shape
    assert gb * S5_GROUP == LANES
    nlb = d // LANES
    jn = seq // r
    bj = batch * jn
    half = gb // 2 * LANES
    tin, tout, toep, lam = s5_prep(lam_re, lam_im, log_step, b_re, b_im, c_re, c_im, r=r, gb=gb)
    lane_block = pl.BlockSpec((m, LANES), lambda lb: (0, lb))
    whole = lambda *shape: pl.BlockSpec((None,) + shape, lambda lb: (lb,) + (0,) * len(shape))
    kern = functools.partial(_s5_kernel, batch=batch, jn=jn, r=r, gb=gb)
    return pl.pallas_call(
        kern,
        grid=(nlb,),
        in_specs=[lane_block,
                  whole(2, r * LANES, LANES), whole(2, r * LANES, LANES), whole(LANES, r * LANES), whole(2, half),
                  pl.BlockSpec((r * LANES, half), lambda lb: (0, 0)),
                  pl.BlockSpec((1, LANES), lambda lb: (0, lb))],
        out_specs=lane_block,
        out_shape=jax.ShapeDtypeStruct((m, d), BF16),
        scratch_shapes=[pltpu.VMEM((m, LANES), F32),
                        pltpu.VMEM((S5_HOP, m // S5_HOP, LANES), F32), pltpu.VMEM((S5_HOP, m // S5_HOP, LANES), F32),
                        pltpu.VMEM((r * LANES, 2 * half), BF16), pltpu.VMEM((r * LANES, 2 * half), BF16),
                        pltpu.VMEM((r * LANES, r * LANES), BF16),
                        pltpu.VMEM((bj, 2 * half), F32), pltpu.VMEM((bj, 2 * half), F32)],
        compiler_params=_cparams(("parallel",)),
        name="s5_blocks",
    )(a_norm, tin, tout, toep, lam, _s5_pair_mask(r, gb), d_skip.reshape(1, d))


def kernel(x, norm_gains, w_in, w_gate_up, b_gate, gla_norm_gain, w_out, s5_lambda_re, s5_lambda_im,
           s5_log_step, s5_b_re, s5_b_im, s5_c_re, s5_c_im, s5_d, w_glu, w_ffn_in, w_ffn_out):
    batch, seq, d = x.shape
    m = batch * seq
    depth = norm_gains.shape[0]
    sb_w = SB_HEADS * SB_HEAD_DIM

    proj_w = -(-w_in.shape[2] // PROJ_TN) * PROJ_TN
    w_in_b = jnp.concatenate(
        [w_in.astype(BF16), jnp.zeros(w_in.shape[:2] + (proj_w - w_in.shape[2],), BF16)], axis=2)
    w_out_b, w_glu_b, w_ffn_out_b = w_out.astype(BF16), w_glu.astype(BF16), w_ffn_out.astype(BF16)

    h = x.astype(F32).reshape(m, d)
    a = rms_norm_bf16(h, norm_gains[0, 0])
    for layer in range(depth):
        gains = norm_gains[layer]
        i = layer // 2
        if layer % 2 == 0:
            proj = matmul(a, w_in_b, i, proj_w, BF16, tm=2048, tn=PROJ_TN)
            o_sb = sb_attention(proj, batch, seq)
            w_gate_pad = jnp.pad(w_gate_up[i], ((0, LANES - GLA_GATE_RANK), (0, 0)))
            o_gla = gla_attention(proj, w_gate_pad, b_gate[i], gla_norm_gain[i], batch, seq, 3 * sb_w)
            h, a = matmul_residual((o_sb, o_gla), w_out_b, i, h, gains[1], gains[2])
        else:
            y = s5_mixer_gelu(a, batch, seq, s5_lambda_re[i], s5_lambda_im[i], s5_log_step[i],
                              s5_b_re[i], s5_b_im[i], s5_c_re[i], s5_c_im[i], s5_d[i])
            h, a = matmul_residual(y, w_glu_b, i, h, gains[1], gains[2], glu=True, tm=1024, tk=d // 4,
                                   manual_h=True)
        f = ffn_in(a, w_ffn_in, layer)
        g_next = norm_gains[layer + 1, 0] if layer + 1 < depth else gains[3]
        h, a = matmul_residual(f, w_ffn_out_b, layer, h, gains[3], g_next, tm=1024,
                               tk=w_ffn_out.shape[1] // 4, manual_h=True)
    return h.reshape(batch, seq, d).astype(x.dtype)
```

```python
import functools
import math

import numpy as np
import jax
import jax.numpy as jnp
from jax import lax
from jax.experimental import pallas as pl
from jax.experimental.pallas import tpu as pltpu

F32 = jnp.float32
BF16 = jnp.bfloat16

NORM_EPS = 1e-6
LANES = 128
SB_HEADS = 8
SB_HEAD_DIM = 128
GLA_HEADS = 8
GLA_DK = 64
GLA_DV = 128
GLA_GATE_RANK = 16
GLA_GATE_TAU = 16.0
S5_GROUP = 16
S5_STATE = 64

SB_TQ = 1024
SB_TK = 256
GLA_CHUNK = 128
PROJ_TN = 1280
S5_R = 16
S5_GB = 8
S5_TOEP_SPLIT = 4
S5_HOP = 4

VMEM_LIMIT = 56 * 1024 * 1024

LOG2E = math.log2(math.e)


def _cparams(sem):
    return pltpu.CompilerParams(dimension_semantics=sem, vmem_limit_bytes=VMEM_LIMIT)


def _rms(x, g):
    return x * lax.rsqrt(jnp.mean(x * x, axis=-1, keepdims=True) + NORM_EPS) * g


def _dot(a, b):
    return jnp.dot(a, b, preferred_element_type=F32)


def _dot_nt(a, b):
    return lax.dot_general(a, b, (((1,), (1,)), ((), ())), preferred_element_type=F32)


def _dot_tn(a, b):
    return lax.dot_general(a, b, (((0,), (0,)), ((), ())), preferred_element_type=F32)


def _split_bf16(x):
    hi = x.astype(BF16)
    lo = (x - hi.astype(F32)).astype(BF16)
    return hi, lo


def _norm_kernel(h_ref, g_ref, o_ref):
    o_ref[...] = _rms(h_ref[...], g_ref[...]).astype(o_ref.dtype)


def rms_norm_bf16(h, g, tm=512):
    m, d = h.shape
    return pl.pallas_call(
        _norm_kernel,
        grid=(m // tm,),
        in_specs=[pl.BlockSpec((tm, d), lambda i: (i, 0)),
                  pl.BlockSpec((1, d), lambda i: (0, 0))],
        out_specs=pl.BlockSpec((tm, d), lambda i: (i, 0)),
        out_shape=jax.ShapeDtypeStruct((m, d), BF16),
        compiler_params=_cparams(("parallel",)),
        name="rms_norm",
    )(h, g.reshape(1, d))


def _mm_kernel(a_ref, w_ref, o_ref, wb_ref):
    @pl.when(pl.program_id(1) == 0)
    def _():
        wb_ref[...] = w_ref[...].astype(BF16)

    o_ref[...] = _dot(a_ref[...], wb_ref[...]).astype(o_ref.dtype)


def matmul(a, w, layer, ncols, out_dtype, tm, tn):
    m, k = a.shape
    return pl.pallas_call(
        _mm_kernel,
        grid=(ncols // tn, m // tm),
        in_specs=[pl.BlockSpec((tm, k), lambda j, i: (i, 0)),
                  pl.BlockSpec((None, k, tn), lambda j, i: (layer, 0, j))],
        out_specs=pl.BlockSpec((tm, tn), lambda j, i: (i, j)),
        out_shape=jax.ShapeDtypeStruct((m, ncols), out_dtype),
        scratch_shapes=[pltpu.VMEM((k, tn), BF16)],
        compiler_params=_cparams(("parallel", "arbitrary")),
        name="matmul",
    )(a, w)


def _ffn_in_kernel(a_ref, wg_ref, wu_ref, o_ref, wgb_ref, wub_ref):
    @pl.when(pl.program_id(1) == 0)
    def _():
        wgb_ref[...] = wg_ref[...].astype(BF16)
        wub_ref[...] = wu_ref[...].astype(BF16)

    a = a_ref[...]
    g = _dot(a, wgb_ref[...])
    u = _dot(a, wub_ref[...])
    o_ref[...] = (g * jax.nn.sigmoid(g) * u).astype(o_ref.dtype)


def ffn_in(a, w, layer, tm=2048, tn=512):
    m, k = a.shape
    nf = w.shape[2] // 2
    nj = nf // tn
    return pl.pallas_call(
        _ffn_in_kernel,
        grid=(nj, m // tm),
        in_specs=[pl.BlockSpec((tm, k), lambda j, i: (i, 0)),
                  pl.BlockSpec((None, k, tn), lambda j, i: (layer, 0, j)),
                  pl.BlockSpec((None, k, tn), lambda j, i: (layer, 0, j + nj))],
        out_specs=pl.BlockSpec((tm, tn), lambda j, i: (i, j)),
        out_shape=jax.ShapeDtypeStruct((m, nf), BF16),
        scratch_shapes=[pltpu.VMEM((k, tn), BF16), pltpu.VMEM((k, tn), BF16)],
        compiler_params=_cparams(("parallel", "arbitrary")),
        name="ffn_in",
    )(a, w, w)


def _mm_res_kernel(*refs, nk, glu, d, n_a, tm, manual_h):
    a_refs = refs[:n_a]
    w_ref, h_ref, gpost_ref, gnext_ref, hout_ref, anext_ref, acc_ref = refs[n_a:n_a + 7]
    i = pl.program_id(0)
    k = pl.program_id(1)

    def product():
        if n_a == 1:
            return _dot(a_refs[0][...], w_ref[...])
        out, lo = None, 0
        for ar in a_refs:
            part = _dot(ar[...], w_ref[lo:lo + ar.shape[1], :])
            out = part if out is None else out + part
            lo += ar.shape[1]
        return out

    def finish(h_tile):
        y = acc_ref[...]
        if glu:
            y = y[:, :d] * jax.nn.sigmoid(y[:, d:])
        hn = h_tile + _rms(y, gpost_ref[...])
        anext_ref[...] = _rms(hn, gnext_ref[...]).astype(anext_ref.dtype)
        return hn

    @pl.when(k == 0)
    def _():
        acc_ref[...] = product()

    @pl.when(k > 0)
    def _():
        acc_ref[...] += product()

    if not manual_h:
        @pl.when(k == nk - 1)
        def _():
            hout_ref[...] = finish(h_ref[...])
        return

    hbuf, sems = refs[n_a + 7:]
    nt = pl.num_programs(0)

    def tile_in(t):
        return pltpu.make_async_copy(h_ref.at[pl.ds(pl.multiple_of(t * tm, tm), tm), :], hbuf, sems.at[0])

    def tile_out(t):
        return pltpu.make_async_copy(hbuf, hout_ref.at[pl.ds(pl.multiple_of(t * tm, tm), tm), :], sems.at[1])

    @pl.when(k == 1)
    def _():
        @pl.when(i > 0)
        def _():
            tile_out(i - 1).wait()
        tile_in(i).start()

    @pl.when(k == nk - 1)
    def _():
        tile_in(i).wait()
        hbuf[...] = finish(hbuf[...])
        tile_out(i).start()

        @pl.when(i == nt - 1)
        def _():
            tile_out(i).wait()


def matmul_residual(a, w, layer, h, g_post, g_next, *, glu=False, tm=512, tk=512, manual_h=False):
    a_list = list(a) if isinstance(a, (tuple, list)) else [a]
    n_a = len(a_list)
    m = a_list[0].shape[0]
    kdim, n = w.shape[1:]
    d = h.shape[1]
    if n_a == 1:
        a_specs = [pl.BlockSpec((tm, tk), lambda i, k: (i, k))]
    else:
        tk = kdim
        assert sum(x.shape[1] for x in a_list) == kdim
        a_specs = [pl.BlockSpec((tm, x.shape[1]), lambda i, k: (i, 0)) for x in a_list]
    nk = kdim // tk
    assert nk >= 2 or not manual_h
    kern = functools.partial(_mm_res_kernel, nk=nk, glu=glu, d=d, n_a=n_a, tm=tm, manual_h=manual_h)
    h_spec = pl.BlockSpec(memory_space=pl.ANY) if manual_h else pl.BlockSpec((tm, d), lambda i, k: (i, 0))
    scratch = [pltpu.VMEM((tm, n), F32)]
    if manual_h:
        scratch += [pltpu.VMEM((tm, d), F32), pltpu.SemaphoreType.DMA((2,))]
    return pl.pallas_call(
        kern,
        grid=(m // tm, nk),
        in_specs=a_specs + [
                  pl.BlockSpec((None, tk, n), lambda i, k: (layer, k, 0)),
                  h_spec,
                  pl.BlockSpec((1, d), lambda i, k: (0, 0)),
                  pl.BlockSpec((1, d), lambda i, k: (0, 0))],
        out_specs=[h_spec, pl.BlockSpec((tm, d), lambda i, k: (i, 0))],
        out_shape=[jax.ShapeDtypeStruct((m, d), F32),
                   jax.ShapeDtypeStruct((m, d), BF16)],
        scratch_shapes=scratch,
        compiler_params=_cparams(("arbitrary" if manual_h else "parallel", "arbitrary")),
        name="matmul_residual",
    )(*a_list, w, h, g_post.reshape(1, d), g_next.reshape(1, d))


def _sb_kernel(q_ref, k_ref, v_ref, u_ref, o_ref, acc_ref, c_ref, *, tq, tk, scale):
    i = pl.program_id(2)
    q = (q_ref[...].astype(F32) * scale).astype(BF16)
    u = u_ref[...]
    acc_ref[...] = jnp.zeros_like(acc_ref)
    c_ref[...] = jnp.zeros_like(c_ref)

    diag_keep = (lax.broadcasted_iota(jnp.int32, (tk, tk), 0) <
                 lax.broadcasted_iota(jnp.int32, (tk, tk), 1))

    def run(items):
        def rows(ref, it):
            return ref[pl.ds(pl.multiple_of(it[0] * tk, tk), tk), :]

        def logits(it):
            return _dot_nt(rows(k_ref, it), q[it[2], :])

        def suffix(it, z):
            ls = jnp.minimum(z, 0.0) - jnp.log(1.0 + jnp.exp2(jnp.abs(z) * (-LOG2E)))
            lk = ls - z
            if it[1]:
                lk = jnp.where(diag_keep, lk, 0.0)
            later = _dot(u, lk.astype(BF16))
            return ls, later, later[0:1, :] + lk[0:1, :]

        def weighted(it, ls, later):
            w = jnp.exp(ls + later)
            if it[1]:
                w = jnp.where(diag_keep, w, 0.0)
            return _dot_tn(rows(v_ref, it), w.astype(BF16))

        zs = [logits(it) for it in items]
        sfx = [None] * len(items)
        pvs = [None] * len(items)
        sfx[0] = suffix(items[0], zs[0])
        for n in range(1, len(items)):
            sfx[n] = suffix(items[n], zs[n])
            pvs[n - 1] = weighted(items[n - 1], sfx[n - 1][0], sfx[n - 1][1])
        pvs[-1] = weighted(items[-1], sfx[-1][0], sfx[-1][1])
        for it, (_, _, total), pv in zip(items, sfx, pvs):
            acc_ref[:, it[2]] += jnp.exp(c_ref[:, it[2]]) * pv
            c_ref[:, it[2]] += total

    nstraddle = tq // tk
    items = []
    for s in range(nstraddle - 1, -1, -1):
        items.append((i * nstraddle + s, True, slice(s * tk, (s + 1) * tk)))
        if s + 1 < nstraddle:
            items.append((i * nstraddle + s, False, slice((s + 1) * tk, tq)))
    run(items)

    def full_blocks(first, count):
        run([(first - n, False, slice(0, tq)) for n in range(count)])

    nfull = i * nstraddle
    group = 2 * nstraddle

    @pl.when(nfull % group != 0)
    def _():
        full_blocks(nfull - 1, nstraddle)

    def body(n, carry):
        full_blocks(nfull - nfull % group - 1 - group * n, group)
        return carry

    lax.fori_loop(0, nfull // group, body, 0)
    o_ref[...] = acc_ref[...].T.astype(o_ref.dtype)


def _sb_umat(tk):
    s = np.arange(tk)[:, None]
    j = np.arange(tk)[None, :]
    return jnp.asarray((j > s).astype(np.float32), dtype=BF16)


def sb_attention(proj, batch, seq, tq=SB_TQ, tk=SB_TK):
    d, nh = SB_HEAD_DIM, SB_HEADS
    assert (tq // tk) % 2 == 0
    nq = seq // tq
    kern = functools.partial(_sb_kernel, tq=tq, tk=tk, scale=d ** -0.5)
    return pl.pallas_call(
        kern,
        grid=(batch, nh, nq),
        in_specs=[pl.BlockSpec((tq, d), lambda b, h, i: (b * nq + i, h)),
                  pl.BlockSpec((seq, d), lambda b, h, i: (b, nh + h)),
                  pl.BlockSpec((seq, d), lambda b, h, i: (b, 2 * nh + h)),
                  pl.BlockSpec((tk, tk), lambda b, h, i: (0, 0))],
        out_specs=pl.BlockSpec((tq, d), lambda b, h, i: (b * nq + i, h)),
        out_shape=jax.ShapeDtypeStruct((batch * seq, nh * d), BF16),
        scratch_shapes=[pltpu.VMEM((d, tq), F32), pltpu.VMEM((1, tq), F32)],
        compiler_params=_cparams(("parallel", "parallel", "parallel")),
        name="sb_attention",
    )(proj, proj, proj, _sb_umat(tk))


def _gla_levels(c):
    return int(math.log2(c))


def _gla_masks(c):
    t = np.arange(c)[:, None]
    s = np.arange(c)[None, :]
    masks = []
    for lev in range(_gla_levels(c)):
        half = c >> (lev + 1)
        blk = 2 * half
        masks.append((t // blk == s // blk) & ((t % blk) >= half) & ((s % blk) < half))
    masks.append(t == s)
    m = np.concatenate(masks, axis=0).astype(np.float32)
    return jnp.asarray(np.concatenate([m, m], axis=1))


def _gla_boundary_rows(cum, half):
    c, width = cum.shape
    blk = 2 * half
    sub = 8
    if half >= sub:
        return jnp.concatenate(
            [jnp.broadcast_to(cum[b * blk + half - 1:b * blk + half, :], (blk, width)) for b in range(c // blk)],
            axis=0)
    x = cum.reshape(c // sub, sub, width)
    row = lax.broadcasted_iota(jnp.int32, (c // sub, sub, width), 1)
    out = None
    for b in range(sub // blk):
        piece = jnp.broadcast_to(x[:, b * blk + half - 1:b * blk + half, :], x.shape)
        out = piece if out is None else jnp.where(row >= b * blk, piece, out)
    return out.reshape(c, width)


def _gla_kernel(q_ref, k_ref, v_ref, r_ref, glr_ref, wg_ref, bg_ref, gain_ref, t_ref, m_ref, bm_ref,
                o_ref, st_ref, *, c, nlev):
    @pl.when(pl.program_id(1) == 0)
    def _():
        st_ref[...] = jnp.zeros_like(st_ref)

    dk, dv = GLA_DK, GLA_DV
    npair = GLA_HEADS // 2
    glr = glr_ref[...]
    wg = wg_ref[...]
    wg1 = wg.astype(BF16)
    rest = wg - wg1.astype(F32)
    wg2 = rest.astype(BF16)
    wg3 = (rest - wg2.astype(F32)).astype(BF16)
    logits = _dot(glr, wg1) + _dot(glr, wg2) + _dot(glr, wg3) + bg_ref[...]
    la = jax.nn.log_sigmoid(logits) / GLA_GATE_TAU
    hi, lo = _split_bf16(la)
    cum = _dot(t_ref[...], hi) + _dot(t_ref[...], lo)
    total = cum[c - 1:c, :]
    f_in = jnp.exp(cum)
    f_out = jnp.exp(total - cum)
    f_lev = [jnp.exp(-jnp.abs(cum - _gla_boundary_rows(cum, c >> (lev + 1)))) for lev in range(nlev)]

    q = q_ref[...].astype(F32) * (dk ** -0.5)
    k = k_ref[...].astype(F32)
    qs = [(q * f).astype(BF16) for f in f_lev] + [q.astype(BF16)]
    ks = [(k * f).astype(BF16) for f in f_lev] + [k.astype(BF16)]
    q_in = (q * f_in).astype(BF16)
    k_out = (k * f_out).astype(BF16)
    ones = jnp.ones((c, 2 * dv), BF16)
    head_a = lax.broadcasted_iota(jnp.int32, (c, 2 * dk), 1) < dk
    zero_k = jnp.zeros((c, 2 * dk), BF16)
    zero_v = jnp.zeros((c, dv), BF16)
    owned = [m_ref[lev * c:(lev + 1) * c, :] > 0.5 for lev in range(nlev + 1)]

    for p in range(npair):
        lanes = slice(p * 2 * dk, (p + 1) * 2 * dk)
        wide = slice(p * 2 * dv, (p + 1) * 2 * dv)
        scores = jnp.zeros((c, 2 * c), F32)
        for lev in range(nlev + 1):
            kp = ks[lev][:, lanes]
            kstack = jnp.concatenate([jnp.where(head_a, kp, zero_k), jnp.where(head_a, zero_k, kp)], axis=0)
            scores = jnp.where(owned[lev], _dot_nt(qs[lev][:, lanes], kstack), scores)
        v = v_ref[:, wide]
        v_bd = jnp.concatenate([jnp.concatenate([v[:, :dv], zero_v], axis=1),
                                jnp.concatenate([zero_v, v[:, dv:]], axis=1)], axis=0)
        st = st_ref[p]
        o = _dot(scores.astype(BF16), v_bd) + _dot(q_in[:, lanes], st.astype(BF16))
        tot_col = _dot_tn(hi[:, lanes], ones) + _dot_tn(lo[:, lanes], ones)
        st_ref[p] = jnp.exp(tot_col) * st + bm_ref[...] * _dot_tn(k_out[:, lanes], v)
        for hd in range(2):
            oh = o[:, hd * dv:(hd + 1) * dv]
            oh = oh * lax.rsqrt(jnp.mean(oh * oh, axis=-1, keepdims=True) + NORM_EPS)
            cols = slice(p * 2 * dv + hd * dv, p * 2 * dv + (hd + 1) * dv)
            rr = r_ref[:, cols].astype(F32)
            o_ref[:, cols] = (oh * gain_ref[:, cols] * (rr * jax.nn.sigmoid(rr))).astype(o_ref.dtype)


def gla_attention(proj, w_gate_pad, b_gate, gain, batch, seq, col0, c=GLA_CHUNK):
    dk, dv, nh = GLA_DK, GLA_DV, GLA_HEADS
    nc = seq // c
    nlev = _gla_levels(c)
    kw, vw = nh * dk, nh * dv
    qb = col0 // kw
    vb = (col0 + 2 * kw) // vw
    lrb = (col0 + 2 * kw + 2 * vw) // LANES
    tincl = jnp.asarray(np.tril(np.ones((c, c), np.float32)), dtype=BF16)
    pair_blocks = jnp.asarray(np.kron(np.eye(2, dtype=np.float32), np.ones((dk, dv), np.float32)))
    kern = functools.partial(_gla_kernel, c=c, nlev=nlev)
    row = lambda b, n: b * nc + n
    const = lambda shape: pl.BlockSpec(shape, lambda b, n: (0, 0))
    return pl.pallas_call(
        kern,
        grid=(batch, nc),
        in_specs=[pl.BlockSpec((c, kw), lambda b, n: (row(b, n), qb)),
                  pl.BlockSpec((c, kw), lambda b, n: (row(b, n), qb + 1)),
                  pl.BlockSpec((c, vw), lambda b, n: (row(b, n), vb)),
                  pl.BlockSpec((c, vw), lambda b, n: (row(b, n), vb + 1)),
                  pl.BlockSpec((c, LANES), lambda b, n: (row(b, n), lrb)),
                  const((LANES, kw)), const((1, kw)), const((1, vw)),
                  const((c, c)), const(((nlev + 1) * c, 2 * c)), const((2 * dk, 2 * dv))],
        out_specs=pl.BlockSpec((c, vw), lambda b, n: (row(b, n), 0)),
        out_shape=jax.ShapeDtypeStruct((batch * seq, vw), BF16),
        scratch_shapes=[pltpu.VMEM((nh // 2, 2 * dk, 2 * dv), F32)],
        compiler_params=_cparams(("parallel", "arbitrary")),
        name="gla",
    )(proj, proj, proj, proj, proj, w_gate_pad, b_gate.reshape(1, -1), gain.reshape(1, -1),
      tincl, _gla_masks(c), pair_blocks)


def _s5_prep_kernel(lr_ref, li_ref, ls_ref, bre_ref, bim_ref, cre_ref, cim_ref,
                    tin_ref, tout_ref, toep_ref, lam_ref, *, gb, r):
    h = S5_GROUP
    r16 = r * h
    first = lax.broadcasted_iota(jnp.int32, (1, LANES), 1) < S5_STATE
    npow = -(-(r + 1) // 8) * 8
    kidx = lax.broadcasted_iota(jnp.int32, (npow, LANES), 0).astype(F32)
    lane_group = (lax.broadcasted_iota(jnp.int32, (h, r * LANES), 1) % LANES) // h

    def tile_rows(x):
        return jnp.broadcast_to(x[None], (r, h, LANES)).reshape(r16, LANES)

    def by_token(x):
        return x.reshape(r, h, LANES).astype(BF16)

    def rows_of(table, powers):
        return jnp.concatenate([jnp.broadcast_to(table[k:k + 1, :], (h, LANES)) for k in powers], axis=0)

    lam_r, lam_i = [], []
    for gi in range(gb):
        lr = lr_ref[gi:gi + 1, :]
        li = li_ref[gi:gi + 1, :]
        dt = jnp.exp(ls_ref[gi:gi + 1, :])
        mag = jnp.exp(kidx * (dt * lr))
        ang = kidx * (dt * li)
        pw_r, pw_i = mag * jnp.cos(ang), mag * jnp.sin(ang)
        lbr, lbi = pw_r[1:2, :], pw_i[1:2, :]
        den = lr * lr + li * li
        nr, ni = lbr - 1.0, lbi
        cr = (nr * lr + ni * li) / den
        ci = (ni * lr - nr * li) / den
        bre, bim = bre_ref[gi], bim_ref[gi]
        bbr = cr * bre - ci * bim
        bbi = cr * bim + ci * bre
        pr, pi = rows_of(pw_r, range(r + 1)), rows_of(pw_i, range(r + 1))
        prr, pir = rows_of(pw_r, range(r - 1, -1, -1)), rows_of(pw_i, range(r - 1, -1, -1))
        bbr_t, bbi_t = tile_rows(bbr), tile_rows(bbi)
        tin_ref[0, :, gi] = by_token(prr * bbr_t - pir * bbi_t)
        tin_ref[1, :, gi] = by_token(prr * bbi_t + pir * bbr_t)
        cr_t, ci_t = tile_rows(cre_ref[gi]), tile_rows(cim_ref[gi])
        pr1, pi1 = pr[h:], pi[h:]
        tout_ref[0, :, gi] = by_token(cr_t * pr1 - ci_t * pi1)
        tout_ref[1, :, gi] = by_token(-(cr_t * pi1 + ci_t * pr1))
        pr0, pi0 = pr[:r16], pi[:r16]
        wk = jnp.where(first, cr_t * pr0 - ci_t * pi0, -(cr_t * pi0 + ci_t * pr0))
        wk_t = jnp.broadcast_to(wk.reshape(r, 1, h, LANES), (r, gb, h, LANES)).reshape(r * gb * h, LANES)
        bb = jnp.where(first, bbr, bbi)
        mt = _dot_nt(bb.astype(BF16), wk_t.astype(BF16))
        toep_ref[gi * h:(gi + 1) * h, :] = jnp.where(lane_group == gi, mt, 0.0).astype(toep_ref.dtype)
        lam_r.append(pr[r * h:r * h + 1, :])
        lam_i.append(pi[r * h:r * h + 1, :])

    def pairs(rows):
        return jnp.concatenate([jnp.where(first, rows[2 * q], rows[2 * q + 1]) for q in range(gb // 2)], axis=1)

    lam_ref[0:1, :] = pairs(lam_r)
    lam_ref[1:2, :] = pairs(lam_i)


def s5_prep(lam_re, lam_im, log_step, b_re, b_im, c_re, c_im, r=S5_R, gb=S5_GB):
    g = lam_re.shape[0]
    h = S5_GROUP
    nlb = g // gb
    dbl = lambda x: jnp.concatenate([x, x], axis=-1)
    lr2, li2 = dbl(lam_re), dbl(lam_im)
    ls2 = jnp.broadcast_to(log_step[:, None], (g, LANES))
    bre2 = dbl(jnp.swapaxes(b_re, 1, 2))
    bim2 = dbl(jnp.swapaxes(b_im, 1, 2))
    cre2, cim2 = dbl(c_re), dbl(c_im)
    vec = pl.BlockSpec((gb, LANES), lambda i: (i, 0))
    mat = pl.BlockSpec((gb, h, LANES), lambda i: (i, 0, 0))
    tspec = pl.BlockSpec((None, 2, r, gb, h, LANES), lambda i: (i, 0, 0, 0, 0, 0))
    tshape = jax.ShapeDtypeStruct((nlb, 2, r, gb, h, LANES), BF16)
    kern = functools.partial(_s5_prep_kernel, gb=gb, r=r)
    tin, tout, toep, lam = pl.pallas_call(
        kern,
        grid=(nlb,),
        in_specs=[vec, vec, vec, mat, mat, mat, mat],
        out_specs=[tspec, tspec,
                   pl.BlockSpec((None, gb * h, r * LANES), lambda i: (i, 0, 0)),
                   pl.BlockSpec((None, 2, gb // 2 * LANES), lambda i: (i, 0, 0))],
        out_shape=[tshape, tshape,
                   jax.ShapeDtypeStruct((nlb, gb * h, r * LANES), BF16),
                   jax.ShapeDtypeStruct((nlb, 2, gb // 2 * LANES), F32)],
        compiler_params=_cparams(("parallel",)),
        name="s5_prep",
    )(lr2, li2, ls2, bre2, bim2, cre2, cim2)
    rows = r * gb * h
    return tin.reshape(nlb, 2, rows, LANES), tout.reshape(nlb, 2, rows, LANES), toep, lam


def _s5_pair_mask(r, gb):
    group = (np.arange(r * LANES) % LANES) // S5_GROUP
    lane = np.arange(gb // 2 * LANES)
    target = 2 * (lane // LANES) + (lane % LANES) // S5_STATE
    return jnp.asarray((group[:, None] == target[None, :]).astype(np.float32), dtype=BF16)


def _s5_kernel(a_ref, tin_ref, tout_ref, toep_ref, lam_ref, mask_ref, d_ref, y_ref,
               tok_scr, in_scr, out_scr, bdin_scr, bdout_scr, bdt_scr, s_scr, xp_scr, *, batch, jn, r, gb):
    bj = batch * jn
    npair = gb // 2
    half = npair * LANES
    hop = S5_HOP
    assert r == hop * hop
    tok_scr[...] = a_ref[...].astype(F32)
    for c in range(hop):
        in_scr[c] = tok_scr[pl.ds(c, bj * hop, stride=hop), :]

    def a_tok(i):
        return in_scr[i % hop, pl.ds(i // hop, bj, stride=hop), :]

    for q in range(npair):
        mq = mask_ref[:, q * LANES:(q + 1) * LANES]
        for part in range(2):
            cols = slice((part * npair + q) * LANES, (part * npair + q + 1) * LANES)
            bdin_scr[:, cols] = tin_ref[part] * mq
            bdout_scr[:, cols] = tout_ref[part] * mq
    for i in range(r):
        if i > 0:
            bdt_scr[i * LANES:(i + 1) * LANES, :i * LANES] = jnp.zeros((LANES, i * LANES), bdt_scr.dtype)
        bdt_scr[i * LANES:(i + 1) * LANES, i * LANES:] = toep_ref[:, :(r - i) * LANES]

    a = jnp.concatenate([a_tok(i).astype(BF16) for i in range(r)], axis=1)
    s_scr[...] = _dot(a, bdin_scr[...])
    lam_r = lam_ref[0:1, :]
    lam_i = lam_ref[1:2, :]

    def step(j, carry):
        out = []
        for b in range(batch):
            xr, xi = carry[2 * b], carry[2 * b + 1]
            row = pl.ds(b * jn + j, 1)
            xp_scr[row, :half] = xr
            xp_scr[row, half:] = xi
            s = s_scr[row, :]
            out.append(lam_r * xr - lam_i * xi + s[:, :half])
            out.append(lam_r * xi + lam_i * xr + s[:, half:])
        return tuple(out)

    zero = jnp.zeros((1, half), F32)
    lax.fori_loop(0, jn, step, (zero,) * (2 * batch), unroll=16)

    y_state = _dot_nt(xp_scr[...].astype(BF16), bdout_scr[...])
    d = d_ref[...]
    span = r // S5_TOEP_SPLIT
    for g in range(S5_TOEP_SPLIT):
        rows = (g + 1) * span * LANES
        cols = slice(g * span * LANES, rows)
        y = y_state[:, cols] + _dot(a[:, :rows], bdt_scr[:rows, cols])
        for i in range(g * span, (g + 1) * span):
            yi = y[:, (i - g * span) * LANES:(i - g * span + 1) * LANES] + a_tok(i) * d
            out_scr[i % hop, pl.ds(i // hop, bj, stride=hop), :] = 0.5 * yi * (1.0 + lax.erf(yi * (2.0 ** -0.5)))
    for c in range(hop):
        tok_scr[pl.ds(c, bj * hop, stride=hop), :] = out_scr[c]
    y_ref[...] = tok_scr[...].astype(y_ref.dtype)


def s5_mixer_gelu(a_norm, batch, seq, lam_re, lam_im, log_step, b_re, b_im, c_re, c_im, d_skip,
                  r=S5_R, gb=S5_GB):
    m, d = a_norm.shape
    assert gb * S5_GROUP == LANES
    nlb = d // LANES
    jn = seq // r
    bj = batch * jn
    half = gb // 2 * LANES
    tin, tout, toep, lam = s5_prep(lam_re, lam_im, log_step, b_re, b_im, c_re, c_im, r=r, gb=gb)
    lane_block = pl.BlockSpec((m, LANES), lambda lb: (0, lb))
    whole = lambda *shape: pl.BlockSpec((None,) + shape, lambda lb: (lb,) + (0,) * len(shape))
    kern = functools.partial(_s5_kernel, batch=batch, jn=jn, r=r, gb=gb)
    return pl.pallas_call(
        kern,
        grid=(nlb,),
        in_specs=[lane_block,
                  whole(2, r * LANES, LANES), whole(2, r * LANES, LANES), whole(LANES, r * LANES), whole(2, half),
                  pl.BlockSpec((r * LANES, half), lambda lb: (0, 0)),
                  pl.BlockSpec((1, LANES), lambda lb: (0, lb))],
        out_specs=lane_block,
        out_shape=jax.ShapeDtypeStruct((m, d), BF16),
        scratch_shapes=[pltpu.VMEM((m, LANES), F32),
                        pltpu.VMEM((S5_HOP, m // S5_HOP, LANES), F32), pltpu.VMEM((S5_HOP, m // S5_HOP, LANES), F32),
                        pltpu.VMEM((r * LANES, 2 * half), BF16), pltpu.VMEM((r * LANES, 2 * half), BF16),
                        pltpu.VMEM((r * LANES, r * LANES), BF16),
                        pltpu.VMEM((bj, 2 * half), F32), pltpu.VMEM((bj, 2 * half), F32)],
        compiler_params=_cparams(("parallel",)),
        name="s5_blocks",
    )(a_norm, tin, tout, toep, lam, _s5_pair_mask(r, gb), d_skip.reshape(1, d))


def kernel(x, norm_gains, w_in, w_gate_up, b_gate, gla_norm_gain, w_out, s5_lambda_re, s5_lambda_im,
           s5_log_step, s5_b_re, s5_b_im, s5_c_re, s5_c_im, s5_d, w_glu, w_ffn_in, w_ffn_out):
    batch, seq, d = x.shape
    m = batch * seq
    depth = norm_gains.shape[0]
    sb_w = SB_HEADS * SB_HEAD_DIM

    proj_w = -(-w_in.shape[2] // PROJ_TN) * PROJ_TN
    w_in_b = jnp.concatenate(
        [w_in.astype(BF16), jnp.zeros(w_in.shape[:2] + (proj_w - w_in.shape[2],), BF16)], axis=2)
    w_out_b, w_glu_b, w_ffn_out_b = w_out.astype(BF16), w_glu.astype(BF16), w_ffn_out.astype(BF16)

    h = x.astype(F32).reshape(m, d)
    a = rms_norm_bf16(h, norm_gains[0, 0])
    for layer in range(depth):
        gains = norm_gains[layer]
        i = layer // 2
        if layer % 2 == 0:
            proj = matmul(a, w_in_b, i, proj_w, BF16, tm=2048, tn=PROJ_TN)
            o_sb = sb_attention(proj, batch, seq)
            w_gate_pad = jnp.pad(w_gate_up[i], ((0, LANES - GLA_GATE_RANK), (0, 0)))
            o_gla = gla_attention(proj, w_gate_pad, b_gate[i], gla_norm_gain[i], batch, seq, 3 * sb_w)
            h, a = matmul_residual((o_sb, o_gla), w_out_b, i, h, gains[1], gains[2])
        else:
            y = s5_mixer_gelu(a, batch, seq, s5_lambda_re[i], s5_lambda_im[i], s5_log_step[i],
                              s5_b_re[i], s5_b_im[i], s5_c_re[i], s5_c_im[i], s5_d[i])
            h, a = matmul_residual(y, w_glu_b, i, h, gains[1], gains[2], glu=True, tm=1024, tk=d // 4,
                                   manual_h=True)
        f = ffn_in(a, w_ffn_in, layer)
        g_next = norm_gains[layer + 1, 0] if layer + 1 < depth else gains[3]
        h, a = matmul_residual(f, w_ffn_out_b, layer, h, gains[3], g_next, tm=1024,
                               tk=w_ffn_out.shape[1] // 4, manual_h=True)
    return h.reshape(batch, seq, d).astype(x.dtype)
```
